```python
import jax, jax.numpy as jnp
from jax import lax
import numpy as np

D_MODEL = 1024
BATCH = 4
SEQ = 4096
DEPTH = 1

GLA_HEADS = 4
GLA_DK = D_MODEL // 8
GLA_DV = D_MODEL // 4
GLA_GATE_RANK = 16
GLA_TAU = 16.0
GLA_CHUNK = 64
NSA_HEADS = 8
NSA_KV_HEADS = 2
NSA_HD = 128
NSA_GROUP = NSA_HEADS // NSA_KV_HEADS
CMP_LEN = 32
CMP_STRIDE = 16
CMP_HIDDEN = 2 * NSA_HD
SEL_BLOCK = 64
SEL_TOPK = 16
SEL_QCHUNK = 64
WIN = 512
WIN_BLOCK = 128
N_EXPERTS = 32
TOP_K = 4
D_EXPERT = D_MODEL
SWIGLU_LIMIT = 7.0
SWIGLU_ALPHA = 1.702
MOE_BLOCK = 256
RMS_EPS = 1e-5

IN_SPLITS = (
    GLA_HEADS * GLA_DK,
    GLA_HEADS * GLA_DK,
    GLA_HEADS * GLA_DV,
    GLA_HEADS * GLA_DV,
    GLA_GATE_RANK,
    NSA_HEADS * NSA_HD,
    NSA_KV_HEADS * NSA_HD, NSA_KV_HEADS * NSA_HD,
    NSA_KV_HEADS * NSA_HD, NSA_KV_HEADS * NSA_HD,
    NSA_KV_HEADS * NSA_HD, NSA_KV_HEADS * NSA_HD,
    NSA_HEADS * 3,
    2 * D_MODEL,
)
IN_COLS = sum(IN_SPLITS)

kernel_name = "hybrid_gla_nsa_moe_block"


def _rmsnorm(x, g):
    xf = x.astype(jnp.float32)
    y = xf * lax.rsqrt(jnp.mean(xf * xf, axis=-1, keepdims=True) + RMS_EPS)
    return (y * g.astype(jnp.float32)).astype(x.dtype)


def _masked_softmax(s, mask):
    s = jnp.where(mask, s, -jnp.inf)
    m = jnp.max(s, axis=-1, keepdims=True)
    m = jnp.where(jnp.isfinite(m), m, 0.0)
    e = jnp.where(mask, jnp.exp(s - m), 0.0)
    return e / jnp.maximum(jnp.sum(e, axis=-1, keepdims=True), 1e-30)


def _gla(q, k, v, log_a):
    B, T, H, dk = q.shape
    dv = v.shape[-1]
    C = GLA_CHUNK
    N = T // C

    def chunks(t):
        return t.reshape(B, N, C, H, t.shape[-1]).transpose(0, 3, 1, 2, 4)

    qc = chunks(q) * dk ** -0.5
    kc = chunks(k)
    vc = chunks(v)
    bcum = jnp.cumsum(chunks(log_a).astype(jnp.float32), axis=3)
    qe = qc * jnp.exp(bcum)
    ke = kc * jnp.exp(-bcum)
    causal = jnp.tril(jnp.ones((C, C), dtype=bool))
    A = jnp.where(causal, jnp.einsum('bhnid,bhnjd->bhnij', qe, ke), 0.0)
    intra = jnp.einsum('bhnij,bhnjv->bhniv', A, vc)
    blast = bcum[:, :, :, -1:, :]
    kdec = kc * jnp.exp(blast - bcum)
    cstate = jnp.einsum('bhncd,bhncv->nbhdv', kdec, vc)
    cdecay = jnp.exp(blast[:, :, :, 0, :]).transpose(2, 0, 1, 3)

    def step(S, inp):
        dec, cs = inp
        return dec[..., None] * S + cs, S

    _, s_prev = lax.scan(step, jnp.zeros((B, H, dk, dv), jnp.float32), (cdecay, cstate))
    inter = jnp.einsum('bhnid,nbhdv->bhniv', qe, s_prev)
    o = intra + inter
    return o.transpose(0, 2, 3, 1, 4).reshape(B, T, H, dv).astype(v.dtype)


def _compress(k, pe, w1, w2):
    B, G, T, hd = k.shape
    ncmp = (T - CMP_LEN) // CMP_STRIDE + 1
    idx = jnp.arange(ncmp)[:, None] * CMP_STRIDE + jnp.arange(CMP_LEN)[None, :]
    blocks = k[:, :, idx] + pe
    flat = blocks.reshape(B, G, ncmp, CMP_LEN * hd)
    return jax.nn.gelu(flat @ w1) @ w2


def _nsa_selected(q, k, v, blk_idx, blk_ok):
    B, G, R, T, hd = q.shape
    n = blk_idx.shape[-1]
    QS = SEL_QCHUNK
    nq = T // QS
    kb = k.reshape(B, G, T // SEL_BLOCK, SEL_BLOCK, hd)
    vb = v.reshape(B, G, T // SEL_BLOCK, SEL_BLOCK, hd)
    bi = jnp.arange(B)[:, None, None, None]
    gi = jnp.arange(G)[None, :, None, None]
    offs = jnp.arange(SEL_BLOCK)

    def one_chunk(args):
        qc, ic, okc, pos = args
        kg = kb[bi, gi, ic]
        vg = vb[bi, gi, ic]
        s = jnp.einsum('bgrqd,bgqnld->bgrqnl', qc, kg).astype(jnp.float32) * hd ** -0.5
        kpos = ic[..., None] * SEL_BLOCK + offs
        mask = okc[..., None] & (kpos <= pos[:, None, None])
        mask = mask.reshape(B, G, 1, QS, n * SEL_BLOCK)
        p = _masked_softmax(s.reshape(B, G, R, QS, n * SEL_BLOCK), mask)
        p = p.reshape(B, G, R, QS, n, SEL_BLOCK)
        return jnp.einsum('bgrqnl,bgqnld->bgrqd', p.astype(vg.dtype), vg)

    xs = (jnp.moveaxis(q.reshape(B, G, R, nq, QS, hd), 3, 0),
          jnp.moveaxis(blk_idx.reshape(B, G, nq, QS, n), 2, 0),
          jnp.moveaxis(blk_ok.reshape(B, G, nq, QS, n), 2, 0),
          jnp.arange(T).reshape(nq, QS))
    out = lax.map(one_chunk, xs)
    return jnp.moveaxis(out, 0, 3).reshape(B, G, R, T, hd)


def _nsa_window(q, k, v):
    B, G, R, T, hd = q.shape
    nb = T // WIN_BLOCK
    nw = WIN // WIN_BLOCK
    pad = ((0, 0), (0, 0), (WIN, 0), (0, 0))
    kp = jnp.pad(k, pad).reshape(B, G, nb + nw, WIN_BLOCK, hd)
    vp = jnp.pad(v, pad).reshape(B, G, nb + nw, WIN_BLOCK, hd)
    kw = jnp.concatenate([kp[:, :, i:i + nb] for i in range(nw + 1)], axis=3)
    vw = jnp.concatenate([vp[:, :, i:i + nb] for i in range(nw + 1)], axis=3)
    qb = q.reshape(B, G, R, nb, WIN_BLOCK, hd)
    s = jnp.einsum('bgrnqd,bgnkd->bgrnqk', qb, kw).astype(jnp.float32) * hd ** -0.5
    qpos = jnp.arange(nb)[:, None, None] * WIN_BLOCK + jnp.arange(WIN_BLOCK)[None, :, None]
    kpos = jnp.arange(nb)[:, None, None] * WIN_BLOCK - WIN + jnp.arange((nw + 1) * WIN_BLOCK)[None, None, :]
    mask = (kpos >= 0) & (kpos <= qpos) & (qpos - kpos < WIN)
    p = _masked_softmax(s, mask)
    o = jnp.einsum('bgrnqk,bgnkd->bgrnqd', p.astype(vw.dtype), vw)
    return o.reshape(B, G, R, T, hd)


def _nsa(q, k_c, v_c, k_s, v_s, k_w, v_w, gates, cmp_pe, cmp_w1, cmp_w2):
    T, hd = q.shape[3], q.shape[4]
    ncmp = (T - CMP_LEN) // CMP_STRIDE + 1
    nsel = T // SEL_BLOCK
    n_top = min(SEL_TOPK, nsel)
    tpos = jnp.arange(T)
    kc = _compress(k_c, cmp_pe[0], cmp_w1[0], cmp_w2[0])
    vc = _compress(v_c, cmp_pe[1], cmp_w1[1], cmp_w2[1])
    s = jnp.einsum('bgrtd,bgcd->bgrtc', q, kc).astype(jnp.float32) * hd ** -0.5
    cmask = (jnp.arange(ncmp) * CMP_STRIDE + CMP_LEN - 1)[None, :] <= tpos[:, None]
    p_cmp = _masked_softmax(s, cmask)
    o_cmp = jnp.einsum('bgrtc,bgcd->bgrtd', p_cmp.astype(vc.dtype), vc)
    cstart = jnp.arange(ncmp) * CMP_STRIDE
    sstart = jnp.arange(nsel) * SEL_BLOCK
    overlap = ((cstart[:, None] < sstart[None, :] + SEL_BLOCK)
               & (cstart[:, None] + CMP_LEN > sstart[None, :])).astype(jnp.float32)
    imp = jnp.einsum('bgtc,cs->bgts', jnp.sum(p_cmp, axis=2), overlap)
    jblk = jnp.arange(nsel)[None, :]
    tblk = (tpos // SEL_BLOCK)[:, None]
    forced = (jblk == 0) | (jblk == tblk) | (jblk == tblk - 1)
    valid = jblk * SEL_BLOCK <= tpos[:, None]
    score = jnp.where(forced, jnp.inf, jnp.where(valid, imp, -jnp.inf))
    top_val, top_idx = lax.top_k(score, n_top)
    o_sel = _nsa_selected(q, k_s, v_s, top_idx, top_val > -jnp.inf)
    o_win = _nsa_window(q, k_w, v_w)
    return gates[..., 0:1] * o_cmp + gates[..., 1:2] * o_sel + gates[..., 2:3] * o_win


def _hybrid_mixer(u, w_in, gla_wa2, gla_ba, gla_norm_g, cmp_pe, cmp_w1, cmp_w2, proj_gla, proj_nsa, w_out):
    B, T, _ = u.shape
    proj = u @ w_in
    offs = np.cumsum(IN_SPLITS)[:-1].tolist()
    (g_q, g_k, g_v, g_r, g_a, n_q, n_kc, n_vc, n_ks, n_vs, n_kw, n_vw, n_g, m_g) = jnp.split(proj, offs, axis=-1)
    log_a = jax.nn.log_sigmoid((g_a @ gla_wa2 + gla_ba).astype(jnp.float32)) / GLA_TAU
    o_gla = _gla(g_q.reshape(B, T, GLA_HEADS, GLA_DK), g_k.reshape(B, T, GLA_HEADS, GLA_DK),
                 g_v.reshape(B, T, GLA_HEADS, GLA_DV), log_a.reshape(B, T, GLA_HEADS, GLA_DK))
    y_gla = (_rmsnorm(o_gla, gla_norm_g) * jax.nn.silu(g_r.reshape(B, T, GLA_HEADS, GLA_DV)))
    y_gla = y_gla.reshape(B, T, GLA_HEADS * GLA_DV)
    qn = n_q.reshape(B, T, NSA_KV_HEADS, NSA_GROUP, NSA_HD).transpose(0, 2, 3, 1, 4)

    def kv(t):
        return t.reshape(B, T, NSA_KV_HEADS, NSA_HD).transpose(0, 2, 1, 3)

    gates = jax.nn.sigmoid(n_g).reshape(B, T, NSA_KV_HEADS, NSA_GROUP, 3).transpose(0, 2, 3, 1, 4)
    o_nsa = _nsa(qn, kv(n_kc), kv(n_vc), kv(n_ks), kv(n_vs), kv(n_kw), kv(n_vw), gates, cmp_pe, cmp_w1, cmp_w2)
    y_nsa = o_nsa.transpose(0, 3, 1, 2, 4).reshape(B, T, NSA_HEADS * NSA_HD)
    mg = jax.nn.sigmoid(m_g).reshape(B, T, 2, D_MODEL)
    merged = mg[:, :, 0] * (y_gla @ proj_gla) + mg[:, :, 1] * (y_nsa @ proj_nsa)
    return merged @ w_out


def _moe(u, router_w, router_b, w1, b1, w2, b2):
    B, T, D = u.shape
    N = B * T
    A = N * TOP_K
    M = MOE_BLOCK
    xt = u.reshape(N, D)
    logits = (xt @ router_w + router_b).astype(jnp.float32)
    top_v, top_e = lax.top_k(logits, TOP_K)
    wts = jax.nn.softmax(top_v, axis=-1)
    flat_e = top_e.reshape(-1)
    flat_t = jnp.repeat(jnp.arange(N, dtype=jnp.int32), TOP_K)
    flat_w = wts.reshape(-1)
    order = jnp.argsort(flat_e)
    se, st, sw = flat_e[order], flat_t[order], flat_w[order]
    counts = jnp.bincount(flat_e, length=N_EXPERTS)
    starts = jnp.cumsum(counts) - counts
    padded = ((counts + M - 1) // M) * M
    pends = jnp.cumsum(padded)
    pstarts = pends - padded
    dest = pstarts[se] + jnp.arange(A) - starts[se]
    P = A + N_EXPERTS * M
    nblk = P // M
    row_tok = jnp.full((P,), N, jnp.int32).at[dest].set(st)
    row_w = jnp.zeros((P,), jnp.float32).at[dest].set(sw)
    blk_e = jnp.minimum(jnp.searchsorted(pends, jnp.arange(nblk) * M, side='right'), N_EXPERTS - 1)
    xpad = jnp.concatenate([xt, jnp.zeros((1, D), xt.dtype)], axis=0)

    def expert_block(args):
        e, toks = args
        hb = xpad[toks] @ w1[e] + b1[e]
        gate, up = hb[:, :D_EXPERT], hb[:, D_EXPERT:]
        gate = jnp.minimum(gate, SWIGLU_LIMIT)
        up = jnp.clip(up, -SWIGLU_LIMIT, SWIGLU_LIMIT)
        act = gate * jax.nn.sigmoid(SWIGLU_ALPHA * gate) * (up + 1.0)
        return act @ w2[e] + b2[e]

    yb = lax.map(expert_block, (blk_e, row_tok.reshape(nblk, M)))
    yb = yb.reshape(P, D)
    y = jnp.zeros((N + 1, D), yb.dtype).at[row_tok].add(yb * row_w[:, None].astype(yb.dtype))
    return y[:N].reshape(B, T, D)


def setup_inputs(seed: int = 0) -> dict:
    key = jax.random.key(seed)
    ks = jax.random.split(key, 24)
    D, L, E, F = D_MODEL, DEPTH, N_EXPERTS, D_EXPERT

    def nrm(k, shape, scale):
        return jax.random.normal(k, shape, jnp.float32) * scale

    return {
        'x': nrm(ks[0], (BATCH, SEQ, D), 1.0),
        'c': nrm(ks[1], (BATCH, D), 1.0),
        'w_ada': nrm(ks[2], (L, D, 6 * D), D ** -0.5),
        'b_ada': nrm(ks[3], (L, 6 * D), 0.02),
        'norm1_g': 1.0 + nrm(ks[4], (L, D), 0.02),
        'w_in': nrm(ks[5], (L, D, IN_COLS), D ** -0.5),
        'gla_wa2': nrm(ks[6], (L, GLA_GATE_RANK, GLA_HEADS * GLA_DK), GLA_GATE_RANK ** -0.5),
        'gla_ba': nrm(ks[7], (L, GLA_HEADS * GLA_DK), 0.1),
        'gla_norm_g': 1.0 + nrm(ks[8], (L, GLA_DV), 0.02),
        'cmp_pe': nrm(ks[9], (L, 2, CMP_LEN, NSA_HD), 0.1),
        'cmp_w1': nrm(ks[10], (L, 2, CMP_LEN * NSA_HD, CMP_HIDDEN), (CMP_LEN * NSA_HD) ** -0.5),
        'cmp_w2': nrm(ks[11], (L, 2, CMP_HIDDEN, NSA_HD), CMP_HIDDEN ** -0.5),
        'proj_gla': nrm(ks[12], (L, GLA_HEADS * GLA_DV, D), (GLA_HEADS * GLA_DV) ** -0.5),
        'proj_nsa': nrm(ks[13], (L, NSA_HEADS * NSA_HD, D), (NSA_HEADS * NSA_HD) ** -0.5),
        'w_out': nrm(ks[14], (L, D, D), D ** -0.5),
        'norm2_g': 1.0 + nrm(ks[15], (L, D), 0.02),
        'router_w': nrm(ks[16], (L, D, E), D ** -0.5),
        'router_b': nrm(ks[17], (L, E), 0.01),
        'moe_w1': nrm(ks[18], (L, E, D, 2 * F), D ** -0.5),
        'moe_b1': nrm(ks[19], (L, E, 2 * F), 0.02),
        'moe_w2': nrm(ks[20], (L, E, F, D), F ** -0.5),
        'moe_b2': nrm(ks[21], (L, E, D), 0.02),
        'final_g': 1.0 + nrm(ks[22], (D,), 0.02),
    }


def reference(x, c, w_ada, b_ada, norm1_g, w_in, gla_wa2, gla_ba, gla_norm_g, cmp_pe, cmp_w1, cmp_w2,
              proj_gla, proj_nsa, w_out, norm2_g, router_w, router_b, moe_w1, moe_b1, moe_w2, moe_b2, final_g):
    B = x.shape[0]
    h = x
    cond = jax.nn.silu(c)
    for l in range(DEPTH):
        mod = (cond @ w_ada[l] + b_ada[l]).reshape(B, 6, 1, D_MODEL)
        shift1, scale1, gate1, shift2, scale2, gate2 = [mod[:, i] for i in range(6)]
        u1 = _rmsnorm(h, norm1_g[l]) * (1.0 + scale1) + shift1
        h = h + gate1 * _hybrid_mixer(u1, w_in[l], gla_wa2[l], gla_ba[l], gla_norm_g[l], cmp_pe[l], cmp_w1[l],
                                      cmp_w2[l], proj_gla[l], proj_nsa[l], w_out[l])
        u2 = _rmsnorm(h, norm2_g[l]) * (1.0 + scale2) + shift2
        h = h + gate2 * _moe(u2, router_w[l], router_b[l], moe_w1[l], moe_b1[l], moe_w2[l], moe_b2[l])
    return _rmsnorm(h, final_g)
```

```python
import functools

import jax
import jax.numpy as jnp
from jax import lax
from jax.experimental import pallas as pl
from jax.experimental.pallas import tpu as pltpu

F32 = jnp.float32
BF16 = jnp.bfloat16

D_MODEL = 1024
GLA_HEADS = 4
GLA_DK = 128
GLA_DV = 256
GLA_GATE_RANK = 16
GLA_TAU = 16.0
GLA_CHUNK = 64
NSA_HEADS = 8
NSA_KV_HEADS = 2
NSA_HD = 128
NSA_GROUP = NSA_HEADS // NSA_KV_HEADS
CMP_LEN = 32
CMP_STRIDE = 16
CMP_HIDDEN = 2 * NSA_HD
SEL_BLOCK = 64
SEL_TOPK = 16
WIN = 512
N_EXPERTS = 32
TOP_K = 4
D_EXPERT = D_MODEL
SWIGLU_LIMIT = 7.0
SWIGLU_ALPHA = 1.702
RMS_EPS = 1e-5

LANES = 128
NEG = -1e30
MOE_ROWS = 256
VMEM_LIMIT = 56 * 1024 * 1024

C_GQ, C_GK, C_GV, C_GR, C_NQ, C_MG, C_KV = 0, 512, 1024, 2048, 3072, 4096, 6144
MAIN_COLS = 7680
S_GA, S_NG = 0, 16

_O_GA = 3072
_O_NQ = 3088
_O_KV = 4112
_O_NG = 5648
_O_MG = 5672
_O_END = 7720


def _dot(a, b):
    return jnp.dot(a, b, preferred_element_type=F32)


def _dot_nt(a, b):
    return lax.dot_general(a, b, (((1,), (1,)), ((), ())), preferred_element_type=F32)


def _dot_tn(a, b):
    return lax.dot_general(a, b, (((0,), (0,)), ((), ())), preferred_element_type=F32)


def _split(a):
    hi = a.astype(BF16)
    lo = (a - hi.astype(F32)).astype(BF16)
    return hi, lo


def _dot3(a, b):
    ah, al = _split(a)
    bh, bl = _split(b)
    return _dot(ah, bh) + _dot(ah, bl) + _dot(al, bh)


def _sigmoid(x):
    return 1.0 / (1.0 + jnp.exp(-x))


def _params(*sem):
    return pltpu.CompilerParams(dimension_semantics=sem, vmem_limit_bytes=VMEM_LIMIT)


def _mod_kernel(c_ref, w_ref, b_ref, o_ref):
    c = c_ref[...]
    cond = c * _sigmoid(c)
    o_ref[...] = _dot3(cond, w_ref[...]) + b_ref[...]


def _mod_call(c8, w_ada, b_ada):
    d = c8.shape[1]
    n = w_ada.shape[1]
    return pl.pallas_call(
        _mod_kernel,
        grid=(n // d,),
        in_specs=[pl.BlockSpec((8, d), lambda j: (0, 0)),
                  pl.BlockSpec((d, d), lambda j: (0, j)),
                  pl.BlockSpec((1, d), lambda j: (0, j))],
        out_specs=pl.BlockSpec((8, d), lambda j: (0, j)),
        out_shape=jax.ShapeDtypeStruct((8, n), F32),
        compiler_params=_params("parallel"),
        name="mod",
    )(c8, w_ada, b_ada)


def _inproj_kernel(x_ref, g_ref, sc_ref, sh_ref, w_ref, ws_ref, o_ref, s_ref, u_sc):
    @pl.when(pl.program_id(1) == 0)
    def _():
        xf = x_ref[...]
        ms = jnp.mean(xf * xf, axis=-1, keepdims=True)
        y = xf * lax.rsqrt(ms + RMS_EPS) * g_ref[...]
        u = (y * (1.0 + sc_ref[0]) + sh_ref[0]).astype(BF16)
        u_sc[...] = u
        s_ref[...] = _dot(u, ws_ref[...])

    o_ref[...] = _dot(u_sc[...], w_ref[...]).astype(BF16)


def _inproj_call(x2, g, scale, shift, w_main, w_small, T, tm=512, tn=1920):
    n, d = x2.shape
    tpb = T // tm
    return pl.pallas_call(
        _inproj_kernel,
        grid=(n // tm, MAIN_COLS // tn),
        in_specs=[pl.BlockSpec((tm, d), lambda i, j: (i, 0)),
                  pl.BlockSpec((1, d), lambda i, j: (0, 0)),
                  pl.BlockSpec((1, 1, d), lambda i, j: (i // tpb, 0, 0)),
                  pl.BlockSpec((1, 1, d), lambda i, j: (i // tpb, 0, 0)),
                  pl.BlockSpec((d, tn), lambda i, j: (0, j)),
                  pl.BlockSpec((d, LANES), lambda i, j: (0, 0))],
        out_specs=[pl.BlockSpec((tm, tn), lambda i, j: (i, j)),
                   pl.BlockSpec((tm, LANES), lambda i, j: (i, 0))],
        out_shape=[jax.ShapeDtypeStruct((n, MAIN_COLS), BF16),
                   jax.ShapeDtypeStruct((n, LANES), F32)],
        scratch_shapes=[pltpu.VMEM((tm, d), BF16)],
        compiler_params=_params("parallel", "arbitrary"),
        name="inproj",
    )(x2, g, scale, shift, w_main, w_small)


def _gla_kernel(q_ref, k_ref, v_ref, r_ref, s_ref, wa_ref, ba_ref, g_ref, o_ref, st_sc, *, tt):
    @pl.when(pl.program_id(2) == 0)
    def _():
        st_sc[...] = jnp.zeros_like(st_sc)

    C = GLA_CHUNK
    ga = s_ref[:, S_GA:S_GA + GLA_GATE_RANK]
    z = _dot3(ga, wa_ref[...]) + ba_ref[...]
    la = (jnp.minimum(z, 0.0) - jnp.log(1.0 + jnp.exp(-jnp.abs(z)))) * (1.0 / GLA_TAU)
    row = lax.broadcasted_iota(jnp.int32, (C, C), 0)
    col = lax.broadcasted_iota(jnp.int32, (C, C), 1)
    causal = col <= row
    tril = jnp.where(causal, 1.0, 0.0).astype(BF16)
    st = st_sc[...]
    for c in range(tt // C):
        sl = slice(c * C, (c + 1) * C)
        hi, lo = _split(la[sl])
        bc = _dot(tril, hi) + _dot(tril, lo)
        bl = bc[C - 1:C, :]
        qc = q_ref[sl, :].astype(F32) * (GLA_DK ** -0.5)
        kc = k_ref[sl, :].astype(F32)
        vc = v_ref[sl, :]
        qe = (qc * jnp.exp(bc)).astype(BF16)
        ke = (kc * jnp.exp(-bc)).astype(BF16)
        kd = (kc * jnp.exp(bl - bc)).astype(BF16)
        a = jnp.where(causal, _dot_nt(qe, ke), 0.0)
        o = _dot(a.astype(BF16), vc) + _dot_nt(qe, st.astype(BF16))
        st = st * jnp.exp(bl) + _dot_tn(vc, kd)
        y = o * lax.rsqrt(jnp.mean(o * o, axis=-1, keepdims=True) + RMS_EPS) * g_ref[...]
        r = r_ref[sl, :].astype(F32)
        o_ref[sl, :] = (y * (r * _sigmoid(r))).astype(BF16)
    st_sc[...] = st


def _gla_call(main, small, wa2, ba, norm_g, B, T, tt=256):
    n = main.shape[0]
    nt = T // tt
    rowmap = lambda b, h, t: b * nt + t
    return pl.pallas_call(
        functools.partial(_gla_kernel, tt=tt),
        grid=(B, GLA_HEADS, nt),
        in_specs=[pl.BlockSpec((tt, GLA_DK), lambda b, h, t: (rowmap(b, h, t), C_GQ // GLA_DK + h)),
                  pl.BlockSpec((tt, GLA_DK), lambda b, h, t: (rowmap(b, h, t), C_GK // GLA_DK + h)),
                  pl.BlockSpec((tt, GLA_DV), lambda b, h, t: (rowmap(b, h, t), C_GV // GLA_DV + h)),
                  pl.BlockSpec((tt, GLA_DV), lambda b, h, t: (rowmap(b, h, t), C_GR // GLA_DV + h)),
                  pl.BlockSpec((tt, LANES), lambda b, h, t: (rowmap(b, h, t), 0)),
                  pl.BlockSpec((GLA_GATE_RANK, GLA_DK), lambda b, h, t: (0, h)),
                  pl.BlockSpec((1, GLA_DK), lambda b, h, t: (0, h)),
                  pl.BlockSpec((1, GLA_DV), lambda b, h, t: (0, 0))],
        out_specs=pl.BlockSpec((tt, GLA_DV), lambda b, h, t: (rowmap(b, h, t), h)),
        out_shape=jax.ShapeDtypeStruct((n, GLA_HEADS * GLA_DV), BF16),
        scratch_shapes=[pltpu.VMEM((GLA_DV, GLA_DK), F32)],
        compiler_params=_params("parallel", "parallel", "arbitrary"),
        name="gla",
    )(main, main, main, main, small, wa2, ba, norm_g)


def _compress_kernel(x_ref, pe_ref, w1_ref, w2_ref, o_ref):
    half = CMP_STRIDE * NSA_HD
    x = x_ref[0, 0]
    nb = x.shape[0]
    ylo = _dot(x, w1_ref[0, :half, :])
    yhi = _dot(x, w1_ref[0, half:, :])
    pe = jnp.broadcast_to(pe_ref[0], (8, 2 * half)).astype(BF16)
    pterm = _dot(pe, w1_ref[0])[0:1, :]
    pre = ylo + pltpu.roll(yhi, nb - 1, 0) + pterm
    hcu = pre * pre * pre
    hid = 0.5 * pre * (1.0 + jnp.tanh(0.7978845608028654 * (pre + 0.044715 * hcu)))
    o_ref[0, 0] = _dot(hid.astype(BF16), w2_ref[0]).astype(BF16)


def _compress_call(xkv, pe, w1, w2):
    _, bg, nb, wdt = xkv.shape
    return pl.pallas_call(
        _compress_kernel,
        grid=(2, bg),
        in_specs=[pl.BlockSpec((1, 1, nb, wdt), lambda s, i: (s, i, 0, 0)),
                  pl.BlockSpec((1, 1, CMP_LEN * NSA_HD), lambda s, i: (s, 0, 0)),
                  pl.BlockSpec((1, CMP_LEN * NSA_HD, CMP_HIDDEN), lambda s, i: (s, 0, 0)),
                  pl.BlockSpec((1, CMP_HIDDEN, NSA_HD), lambda s, i: (s, 0, 0))],
        out_specs=pl.BlockSpec((1, 1, nb, NSA_HD), lambda s, i: (s, i, 0, 0)),
        out_shape=jax.ShapeDtypeStruct((2, bg, nb, NSA_HD), BF16),
        compiler_params=_params("parallel", "parallel"),
        name="compress",
    )(xkv, pe, w1, w2)


def _cmpsel_kernel(q_ref, kc_ref, vc_ref, o_ref, sel_ref, *, tq, nsel):
    t0 = pl.program_id(2) * tq
    kc = kc_ref[0, 0]
    vc = vc_ref[0, 0]
    ncp = kc.shape[0]
    tpos = t0 + lax.broadcasted_iota(jnp.int32, (tq, 1), 0)
    cidx = lax.broadcasted_iota(jnp.int32, (1, ncp), 1)
    cmask = (cidx * CMP_STRIDE + (CMP_LEN - 1)) <= tpos
    psum = jnp.zeros((tq, ncp), F32)
    for r in range(NSA_GROUP):
        q = q_ref[:, r * NSA_HD:(r + 1) * NSA_HD]
        s = jnp.where(cmask, _dot_nt(q, kc) * (NSA_HD ** -0.5), NEG)
        m = jnp.max(s, axis=-1, keepdims=True)
        e = jnp.where(cmask, jnp.exp(s - m), 0.0)
        p = e / jnp.maximum(jnp.sum(e, axis=-1, keepdims=True), 1e-30)
        o_ref[:, r * NSA_HD:(r + 1) * NSA_HD] = _dot(p.astype(BF16), vc).astype(BF16)
        psum = psum + p
    ci = lax.broadcasted_iota(jnp.int32, (ncp, nsel), 0) * CMP_STRIDE
    si = lax.broadcasted_iota(jnp.int32, (ncp, nsel), 1) * SEL_BLOCK
    ov = jnp.where((ci < si + SEL_BLOCK) & (ci + CMP_LEN > si), 1.0, 0.0).astype(BF16)
    hi, lo = _split(psum)
    imp = _dot(hi, ov) + _dot(lo, ov)
    jblk = lax.broadcasted_iota(jnp.int32, (1, nsel), 1)
    tblk = tpos // SEL_BLOCK
    forced = (jblk == 0) | (jblk == tblk) | (jblk == tblk - 1)
    valid = (jblk * SEL_BLOCK) <= tpos
    score = jnp.where(forced, jnp.inf, jnp.where(valid, imp, -jnp.inf))
    rank = jnp.zeros((tq, nsel), F32)
    for i in range(nsel):
        ci_ = score[:, i:i + 1]
        tie = jnp.where(jblk > i, 1.0, 0.0)
        rank = rank + jnp.where(ci_ > score, 1.0, jnp.where(ci_ == score, tie, 0.0))
    ntop = min(SEL_TOPK, nsel)
    sel_ref[0, 0] = jnp.where(valid & (rank < ntop), 1.0, 0.0).astype(BF16)


def _cmpsel_call(main, kvc, B, T, tq=256):
    n = main.shape[0]
    nq = T // tq
    nsel = T // SEL_BLOCK
    ncp = kvc.shape[2]
    G = NSA_KV_HEADS
    qw = NSA_GROUP * NSA_HD
    return pl.pallas_call(
        functools.partial(_cmpsel_kernel, tq=tq, nsel=nsel),
        grid=(B, G, nq),
        in_specs=[pl.BlockSpec((tq, qw), lambda b, g, t: (b * nq + t, C_NQ // qw + g)),
                  pl.BlockSpec((1, 1, ncp, NSA_HD), lambda b, g, t: (0, b * G + g, 0, 0)),
                  pl.BlockSpec((1, 1, ncp, NSA_HD), lambda b, g, t: (1, b * G + g, 0, 0))],
        out_specs=[pl.BlockSpec((tq, qw), lambda b, g, t: (b * nq + t, g)),
                   pl.BlockSpec((1, 1, tq, nsel), lambda b, g, t: (b, g, t, 0))],
        out_shape=[jax.ShapeDtypeStruct((n, NSA_HEADS * NSA_HD), BF16),
                   jax.ShapeDtypeStruct((B, G, T, nsel), BF16)],
        compiler_params=_params("parallel", "parallel", "parallel"),
        name="cmpsel",
    )(main, kvc, kvc)


def _nsa_kernel(q_ref, ks_ref, vs_ref, kw_ref, vw_ref, sel_ref, oc_ref, s_ref, o_ref, *, tq, tk, T, nsel):
    R = NSA_GROUP
    hd = NSA_HD
    g = pl.program_id(1)
    t0 = pl.program_id(2) * tq
    scale = hd ** -0.5
    q4 = jnp.concatenate([q_ref[:, r * hd:(r + 1) * hd] for r in range(R)], axis=0)
    rowpos = t0 + lax.broadcasted_iota(jnp.int32, (tq, 1), 0)
    sel = sel_ref[0, 0]

    nk = (t0 + tq + tk - 1) // tk
    eb = lax.broadcasted_iota(jnp.int32, (nsel, tk), 0)
    ec = lax.broadcasted_iota(jnp.int32, (nsel, tk), 1) // SEL_BLOCK
    kcol = lax.broadcasted_iota(jnp.int32, (1, tk), 1)

    def body(kt, carry):
        m, l, acc = carry
        k0 = pl.multiple_of(kt * tk, tk)
        k = ks_ref[pl.ds(k0, tk), :]
        v = vs_ref[pl.ds(k0, tk), :]
        expand = jnp.where(eb == ec + kt * (tk // SEL_BLOCK), 1.0, 0.0).astype(BF16)
        chosen = _dot(sel, expand)
        bias = jnp.where((k0 + kcol) <= rowpos, (chosen - 1.0) * 1e30, NEG)
        s = (_dot_nt(q4, k) * scale).reshape(R, tq, tk) + bias[None]
        m_new = jnp.maximum(m, jnp.max(s, axis=-1, keepdims=True))
        alpha = jnp.exp(m - m_new)
        p = jnp.exp(s - m_new)
        l = alpha * l + jnp.sum(p, axis=-1, keepdims=True)
        pv = _dot(p.reshape(R * tq, tk).astype(BF16), v).reshape(R, tq, hd)
        return m_new, l, alpha * acc + pv

    init = (jnp.full((R, tq, 1), NEG, F32), jnp.zeros((R, tq, 1), F32), jnp.zeros((R, tq, hd), F32))
    _, l_s, acc_s = lax.fori_loop(0, nk, body, init)
    o_sel = acc_s / l_s

    wlen = WIN + tq
    w0 = pl.multiple_of(jnp.maximum(t0 - WIN, 0), tq)
    kw = kw_ref[pl.ds(w0, wlen), :]
    vw = vw_ref[pl.ds(w0, wlen), :]
    kpos = w0 + lax.broadcasted_iota(jnp.int32, (1, wlen), 1)
    wbias = jnp.where((kpos <= rowpos) & (rowpos - kpos < WIN), 0.0, NEG)
    sw = (_dot_nt(q4, kw) * scale).reshape(R, tq, wlen) + wbias[None]
    mw = jnp.max(sw, axis=-1, keepdims=True)
    pw = jnp.exp(sw - mw)
    lw = jnp.sum(pw, axis=-1, keepdims=True)
    o_win = _dot(pw.reshape(R * tq, wlen).astype(BF16), vw).reshape(R, tq, hd) / lw

    ng = R * 3
    graw = jnp.where(g == 0, s_ref[:, S_NG:S_NG + ng], s_ref[:, S_NG + ng:S_NG + 2 * ng])
    gates = _sigmoid(graw)
    for r in range(R):
        oc = oc_ref[:, r * hd:(r + 1) * hd].astype(F32)
        out = (gates[:, 3 * r:3 * r + 1] * oc + gates[:, 3 * r + 1:3 * r + 2] * o_sel[r]
               + gates[:, 3 * r + 2:3 * r + 3] * o_win[r])
        o_ref[:, r * hd:(r + 1) * hd] = out.astype(BF16)


def _nsa_call(main, small, sel, o_cmp, B, T, tq=128, tk=512):
    n = main.shape[0]
    nq = T // tq
    nsel = T // SEL_BLOCK
    G = NSA_KV_HEADS
    hd = NSA_HD
    qw = NSA_GROUP * hd
    tk = min(tk, T)
    kvb = C_KV // hd

    def kvspec(slot):
        return pl.BlockSpec((T, hd), lambda b, g, t: (b, kvb + 2 * slot + g))

    return pl.pallas_call(
        functools.partial(_nsa_kernel, tq=tq, tk=tk, T=T, nsel=nsel),
        grid=(B, G, nq),
        in_specs=[pl.BlockSpec((tq, qw), lambda b, g, t: (b * nq + t, C_NQ // qw + g)),
                  kvspec(2), kvspec(3), kvspec(4), kvspec(5),
                  pl.BlockSpec((1, 1, tq, nsel), lambda b, g, t: (b, g, t, 0)),
                  pl.BlockSpec((tq, qw), lambda b, g, t: (b * nq + t, g)),
                  pl.BlockSpec((tq, LANES), lambda b, g, t: (b * nq + t, 0))],
        out_specs=pl.BlockSpec((tq, qw), lambda b, g, t: (b * nq + t, g)),
        out_shape=jax.ShapeDtypeStruct((n, NSA_HEADS * hd), BF16),
        compiler_params=_params("parallel", "parallel", "parallel"),
        name="nsa",
    )(main, main, main, main, main, sel, o_cmp, small)


def _merge_kernel(yg_ref, yn_ref, mg_ref, x_ref, g1_ref, sc_ref, sh_ref, n2_ref, pg_ref, pn_ref, wo_ref,
                  rw_ref, rb_ref, h_ref, u_ref, rt_ref, cnt_ref, carry_sc, *, tm):
    d = D_MODEL

    @pl.when(pl.program_id(0) == 0)
    def _():
        carry_sc[...] = jnp.zeros_like(carry_sc)

    a = _dot(yg_ref[...], pg_ref[...])
    b = _dot(yn_ref[...], pn_ref[...])
    mg = _sigmoid(mg_ref[...].astype(F32))
    merged = mg[:, :d] * a + mg[:, d:] * b
    mix = _dot(merged.astype(BF16), wo_ref[...])
    h = x_ref[...] + g1_ref[0] * mix
    h_ref[...] = h
    y = h * lax.rsqrt(jnp.mean(h * h, axis=-1, keepdims=True) + RMS_EPS) * n2_ref[...]
    u = y * (1.0 + sc_ref[0]) + sh_ref[0]
    u_ref[...] = u

    lane = lax.broadcasted_iota(jnp.int32, (1, LANES), 1).astype(F32)
    logits = jnp.where(lane < N_EXPERTS, _dot3(u, rw_ref[...]) + rb_ref[...], -jnp.inf)
    vals, idxs = [], []
    cur = logits
    for _ in range(TOP_K):
        m = jnp.max(cur, axis=-1, keepdims=True)
        idx = jnp.min(jnp.where(cur == m, lane, float(LANES)), axis=-1, keepdims=True)
        vals.append(m)
        idxs.append(idx)
        cur = jnp.where(lane == idx, -jnp.inf, cur)
    es = [jnp.exp(v - vals[0]) for v in vals]
    den = es[0] + es[1] + es[2] + es[3]
    onehots = [lane == idx for idx in idxs]
    oh = jnp.zeros((tm, LANES), F32)
    for o in onehots:
        oh = oh + jnp.where(o, 1.0, 0.0)
    row = lax.broadcasted_iota(jnp.int32, (tm, tm), 0)
    col = lax.broadcasted_iota(jnp.int32, (tm, tm), 1)
    tril = jnp.where(col <= row, 1.0, 0.0).astype(BF16)
    cnt = _dot(tril, oh.astype(BF16))
    tot = cnt + carry_sc[0:1, :]
    route = jnp.zeros((tm, LANES), F32)
    for k in range(TOP_K):
        pos = jnp.sum(jnp.where(onehots[k], tot - 1.0, 0.0), axis=-1, keepdims=True)
        route = jnp.where(lane == k, es[k] / den, route)
        route = jnp.where(lane == TOP_K + k, idxs[k], route)
        route = jnp.where(lane == 2 * TOP_K + k, pos, route)
    rt_ref[...] = route
    newc = tot[tm - 1:tm, :]
    carry_sc[...] = jnp.broadcast_to(newc, carry_sc.shape)
    cnt_ref[...] = jnp.broadcast_to(newc, cnt_ref.shape)


def _merge_call(y_gla, y_nsa, main, x2, gate1, scale2, shift2, n2g, pg, pn, wo, rw, rb, T, tm=512):
    n, d = x2.shape
    tpb = T // tm
    bmap = lambda i: (i // tpb, 0, 0)
    full = lambda i: (0, 0)
    return pl.pallas_call(
        functools.partial(_merge_kernel, tm=tm),
        grid=(n // tm,),
        in_specs=[pl.BlockSpec((tm, d), lambda i: (i, 0)),
                  pl.BlockSpec((tm, d), lambda i: (i, 0)),
                  pl.BlockSpec((tm, 2 * d), lambda i: (i, C_MG // (2 * d))),
                  pl.BlockSpec((tm, d), lambda i: (i, 0)),
                  pl.BlockSpec((1, 1, d), bmap), pl.BlockSpec((1, 1, d), bmap), pl.BlockSpec((1, 1, d), bmap),
                  pl.BlockSpec((1, d), full),
                  pl.BlockSpec((d, d), full), pl.BlockSpec((d, d), full), pl.BlockSpec((d, d), full),
                  pl.BlockSpec((d, LANES), full), pl.BlockSpec((1, LANES), full)],
        out_specs=[pl.BlockSpec((tm, d), lambda i: (i, 0)),
                   pl.BlockSpec((tm, d), lambda i: (i, 0)),
                   pl.BlockSpec((tm, LANES), lambda i: (i, 0)),
                   pl.BlockSpec((8, LANES), full)],
        out_shape=[jax.ShapeDtypeStruct((n, d), F32), jax.ShapeDtypeStruct((n, d), F32),
                   jax.ShapeDtypeStruct((n, LANES), F32), jax.ShapeDtypeStruct((8, LANES), F32)],
        scratch_shapes=[pltpu.VMEM((8, LANES), F32)],
        compiler_params=_params("arbitrary"),
        name="merge",
    )(y_gla, y_nsa, main, x2, gate1, scale2, shift2, n2g, pg, pn, wo, rw, rb)


def _dispatch_kernel(dest_ref, u_ref, xz_ref, xs_ref, sem, *, td):
    del xz_ref
    base = pl.program_id(0) * (td * TOP_K)

    def row_copy(r, slot):
        return pltpu.make_async_copy(u_ref.at[pl.ds(r, 1)], xs_ref.at[pl.ds(slot, 1)], sem)

    def issue(r, c):
        for k in range(TOP_K):
            row_copy(r, dest_ref[base + r * TOP_K + k]).start()
        return c

    def drain(r, c):
        for k in range(TOP_K):
            row_copy(r, dest_ref[base + r * TOP_K + k]).wait()
        return c

    lax.fori_loop(0, td, issue, 0)
    lax.fori_loop(0, td, drain, 0)


def _dispatch_call(dest, u2, xs_zero, td=256):
    n, d = u2.shape
    return pl.pallas_call(
        functools.partial(_dispatch_kernel, td=td),
        grid_spec=pltpu.PrefetchScalarGridSpec(
            num_scalar_prefetch=1,
            grid=(n // td,),
            in_specs=[pl.BlockSpec((td, d), lambda i, dest: (i, 0)),
                      pl.BlockSpec(memory_space=pl.ANY)],
            out_specs=pl.BlockSpec(memory_space=pl.ANY),
            scratch_shapes=[pltpu.SemaphoreType.DMA(())]),
        out_shape=jax.ShapeDtypeStruct(xs_zero.shape, xs_zero.dtype),
        input_output_aliases={2: 0},
        compiler_params=_params("arbitrary"),
        name="dispatch",
    )(dest, u2, xs_zero)


def _moe_kernel(be_ref, na_ref, x_ref, w1_ref, b1_ref, w2_ref, b2_ref, y_ref, w1b, w2b):
    i = pl.program_id(0)
    e = be_ref[i]
    prev = be_ref[jnp.maximum(i - 1, 0)]
    active = i < na_ref[0]
    f = D_EXPERT

    @pl.when(active & ((i == 0) | (e != prev)))
    def _():
        w1b[...] = w1_ref[0].astype(BF16)
        w2b[...] = w2_ref[0].astype(BF16)

    @pl.when(active)
    def _():
        hb = _dot(x_ref[...].astype(BF16), w1b[...]) + b1_ref[0]
        gate = jnp.minimum(hb[:, :f], SWIGLU_LIMIT)
        up = jnp.clip(hb[:, f:], -SWIGLU_LIMIT, SWIGLU_LIMIT)
        act = gate * _sigmoid(SWIGLU_ALPHA * gate) * (up + 1.0)
        y_ref[...] = _dot(act.astype(BF16), w2b[...]) + b2_ref[0]

    @pl.when(jnp.logical_not(active))
    def _():
        y_ref[...] = jnp.zeros_like(y_ref)


def _moe_call(blk_e, nact, xs, w1, b1, w2, b2):
    p, d = xs.shape
    f2 = w1.shape[2]
    m = MOE_ROWS
    return pl.pallas_call(
        _moe_kernel,
        grid_spec=pltpu.PrefetchScalarGridSpec(
            num_scalar_prefetch=2,
            grid=(p // m,),
            in_specs=[pl.BlockSpec((m, d), lambda i, be, na: (i, 0)),
                      pl.BlockSpec((1, d, f2), lambda i, be, na: (be[i], 0, 0)),
                      pl.BlockSpec((1, 1, f2), lambda i, be, na: (be[i], 0, 0)),
                      pl.BlockSpec((1, f2 // 2, d), lambda i, be, na: (be[i], 0, 0)),
                      pl.BlockSpec((1, 1, d), lambda i, be, na: (be[i], 0, 0))],
            out_specs=pl.BlockSpec((m, d), lambda i, be, na: (i, 0)),
            scratch_shapes=[pltpu.VMEM((d, f2), BF16), pltpu.VMEM((f2 // 2, d), BF16)]),
        out_shape=jax.ShapeDtypeStruct((p, d), F32),
        compiler_params=_params("arbitrary"),
        name="moe",
    )(blk_e, nact, xs, w1, b1, w2, b2)


def _combine_kernel(dest_ref, h_ref, rt_ref, g2_ref, fg_ref, ys_ref, o_ref, buf, sem, *, tc):
    base = pl.program_id(0) * (tc * TOP_K)

    def row_copy(r, k, slot):
        return pltpu.make_async_copy(ys_ref.at[pl.ds(slot, 1)], buf.at[k, pl.ds(r, 1)], sem)

    def issue(r, c):
        for k in range(TOP_K):
            row_copy(r, k, dest_ref[base + r * TOP_K + k]).start()
        return c

    def drain(r, c):
        for k in range(TOP_K):
            row_copy(r, k, dest_ref[base + r * TOP_K + k]).wait()
        return c

    lax.fori_loop(0, tc, issue, 0)
    lax.fori_loop(0, tc, drain, 0)
    rt = rt_ref[...]
    moe = rt[:, 0:1] * buf[0]
    for k in range(1, TOP_K):
        moe = moe + rt[:, k:k + 1] * buf[k]
    hh = h_ref[...] + g2_ref[0] * moe
    o_ref[...] = hh * lax.rsqrt(jnp.mean(hh * hh, axis=-1, keepdims=True) + RMS_EPS) * fg_ref[...]


def _combine_call(dest, h, route, gate2, fg, ys, T, tc=128):
    n, d = h.shape
    tpb = T // tc
    return pl.pallas_call(
        functools.partial(_combine_kernel, tc=tc),
        grid_spec=pltpu.PrefetchScalarGridSpec(
            num_scalar_prefetch=1,
            grid=(n // tc,),
            in_specs=[pl.BlockSpec((tc, d), lambda i, dest: (i, 0)),
                      pl.BlockSpec((tc, LANES), lambda i, dest: (i, 0)),
                      pl.BlockSpec((1, 1, d), lambda i, dest: (i // tpb, 0, 0)),
                      pl.BlockSpec((1, d), lambda i, dest: (0, 0)),
                      pl.BlockSpec(memory_space=pl.ANY)],
            out_specs=pl.BlockSpec((tc, d), lambda i, dest: (i, 0)),
            scratch_shapes=[pltpu.VMEM((TOP_K, tc, d), F32), pltpu.SemaphoreType.DMA(())]),
        out_shape=jax.ShapeDtypeStruct((n, d), F32),
        compiler_params=_params("arbitrary"),
        name="combine",
    )(dest, h, route, gate2, fg, ys)


def _mixer(x2, mod, norm1_g, w_in, gla_wa2, gla_ba, gla_norm_g, cmp_pe, cmp_w1, cmp_w2, B, T):
    d = D_MODEL
    G = NSA_KV_HEADS
    hd = NSA_HD
    shift1 = mod[:, 0 * d:1 * d].reshape(B, 1, d)
    scale1 = mod[:, 1 * d:2 * d].reshape(B, 1, d)
    w_main = jnp.concatenate([w_in[:, :_O_GA], w_in[:, _O_NQ:_O_KV], w_in[:, _O_MG:_O_END],
                              w_in[:, _O_KV:_O_NG]], axis=1).astype(BF16)
    w_small = jnp.concatenate([w_in[:, _O_GA:_O_NQ], w_in[:, _O_NG:_O_MG],
                               jnp.zeros((d, LANES - GLA_GATE_RANK - NSA_HEADS * 3), F32)], axis=1).astype(BF16)
    main, small = _inproj_call(x2, norm1_g.reshape(1, d), scale1, shift1, w_main, w_small, T)
    y_gla = _gla_call(main, small, gla_wa2, gla_ba.reshape(1, -1), gla_norm_g.reshape(1, -1), B, T)
    kv = main[:, C_KV:C_KV + 2 * G * hd].reshape(B, T // CMP_STRIDE, CMP_STRIDE, 2, G, hd)
    xkv = kv.transpose(3, 0, 4, 1, 2, 5).reshape(2, B * G, T // CMP_STRIDE, CMP_STRIDE * hd)
    kvc = _compress_call(xkv, cmp_pe.reshape(2, 1, CMP_LEN * hd), cmp_w1.astype(BF16), cmp_w2.astype(BF16))
    o_cmp, sel = _cmpsel_call(main, kvc, B, T)
    y_nsa = _nsa_call(main, small, sel, o_cmp, B, T)
    return main, y_gla, y_nsa


def _route_tables(route, counts, n):
    m = MOE_ROWS
    e = route[:, TOP_K:2 * TOP_K].astype(jnp.int32)
    pos = route[:, 2 * TOP_K:3 * TOP_K].astype(jnp.int32)
    cnt = counts[0, :N_EXPERTS].astype(jnp.int32)
    padded = ((cnt + m - 1) // m) * m
    pends = jnp.cumsum(padded)
    pstarts = pends - padded
    dest = (pstarts[e] + pos).reshape(-1)
    p_rows = n * TOP_K + N_EXPERTS * m
    nblk = p_rows // m
    blk_e = jnp.minimum(jnp.searchsorted(pends, jnp.arange(nblk, dtype=jnp.int32) * m, side='right'),
                        N_EXPERTS - 1).astype(jnp.int32)
    nact = (pends[-1:] // m).astype(jnp.int32)
    return dest, blk_e, nact, p_rows


def kernel(x, c, w_ada, b_ada, norm1_g, w_in, gla_wa2, gla_ba, gla_norm_g, cmp_pe, cmp_w1, cmp_w2, proj_gla,
           proj_nsa, w_out, norm2_g, router_w, router_b, moe_w1, moe_b1, moe_w2, moe_b2, final_g):
    B, T, d = x.shape
    n = B * T
    assert w_ada.shape[0] == 1 and d == D_MODEL and B <= 8, "single-layer block of width D_MODEL"
    l = 0
    h = x.reshape(n, d)
    c8 = jnp.concatenate([c, jnp.zeros((8 - B, d), c.dtype)], axis=0)
    mod = _mod_call(c8, w_ada[l], b_ada[l].reshape(1, -1))[:B]
    gate1 = mod[:, 2 * d:3 * d].reshape(B, 1, d)
    shift2 = mod[:, 3 * d:4 * d].reshape(B, 1, d)
    scale2 = mod[:, 4 * d:5 * d].reshape(B, 1, d)
    gate2 = mod[:, 5 * d:6 * d].reshape(B, 1, d)
    main, y_gla, y_nsa = _mixer(h, mod, norm1_g[l], w_in[l], gla_wa2[l], gla_ba[l], gla_norm_g[l],
                                cmp_pe[l], cmp_w1[l], cmp_w2[l], B, T)
    rw = jnp.concatenate([router_w[l], jnp.zeros((d, LANES - N_EXPERTS), F32)], axis=1)
    rb = jnp.concatenate([router_b[l], jnp.zeros((LANES - N_EXPERTS,), F32)]).reshape(1, LANES)
    h1, u2, route, counts = _merge_call(
        y_gla, y_nsa, main, h, gate1, scale2, shift2, norm2_g[l].reshape(1, d),
        proj_gla[l].astype(BF16), proj_nsa[l].astype(BF16), w_out[l].astype(BF16), rw, rb, T)
    dest, blk_e, nact, p_rows = _route_tables(route, counts, n)
    xs = _dispatch_call(dest, u2, jnp.zeros((p_rows, d), F32))
    ys = _moe_call(blk_e, nact, xs, moe_w1[l], moe_b1[l].reshape(N_EXPERTS, 1, -1), moe_w2[l],
                   moe_b2[l].reshape(N_EXPERTS, 1, -1))
    out = _combine_call(dest, h1, route, gate2, final_g.reshape(1, d), ys, T)
    return out.reshape(B, T, d)
```

```python
import functools

import jax
import jax.numpy as jnp
from jax import lax
from jax.experimental import pallas as pl
from jax.experimental.pallas import tpu as pltpu

F32 = jnp.float32
BF16 = jnp.bfloat16

D_MODEL = 1024
GLA_HEADS = 4
GLA_DK = 128
GLA_DV = 256
GLA_GATE_RANK = 16
GLA_TAU = 16.0
GLA_CHUNK = 64
NSA_HEADS = 8
NSA_KV_HEADS = 2
NSA_HD = 128
NSA_GROUP = NSA_HEADS // NSA_KV_HEADS
CMP_LEN = 32
CMP_STRIDE = 16
CMP_HIDDEN = 2 * NSA_HD
SEL_BLOCK = 64
SEL_TOPK = 16
WIN = 512
N_EXPERTS = 32
TOP_K = 4
D_EXPERT = D_MODEL
SWIGLU_LIMIT = 7.0
SWIGLU_ALPHA = 1.702
RMS_EPS = 1e-5

LANES = 128
NEG = -1e30
LOG2E = 1.4426950408889634
MOE_ROWS = 256
VMEM_LIMIT = 56 * 1024 * 1024

C_GQ, C_GK, C_GV, C_GR, C_NQ, C_MG, C_KV = 0, 512, 1024, 2048, 3072, 4096, 6144
MAIN_COLS = 7680
S_GA, S_NG = 0, 16

_O_GA = 3072
_O_NQ = 3088
_O_KV = 4112
_O_NG = 5648
_O_MG = 5672
_O_END = 7720


def _dot(a, b):
    return jnp.dot(a, b, preferred_element_type=F32)


def _dot_nt(a, b):
    return lax.dot_general(a, b, (((1,), (1,)), ((), ())), preferred_element_type=F32)


def _dot_tn(a, b):
    return lax.dot_general(a, b, (((0,), (0,)), ((), ())), preferred_element_type=F32)


def _split(a):
    hi = a.astype(BF16)
    lo = (a - hi.astype(F32)).astype(BF16)
    return hi, lo


def _dot3(a, b):
    ah, al = _split(a)
    bh, bl = _split(b)
    return _dot(ah, bh) + _dot(ah, bl) + _dot(al, bh)


def _sigmoid(x):
    return 1.0 / (1.0 + jnp.exp(-x))


def _params(*sem):
    return pltpu.CompilerParams(dimension_semantics=sem, vmem_limit_bytes=VMEM_LIMIT)


def _mod_kernel(c_ref, w_ref, b_ref, o_ref):
    c = c_ref[...]
    cond = c * _sigmoid(c)
    o_ref[...] = _dot3(cond, w_ref[...]) + b_ref[...]


def _mod_call(c8, w_ada, b_ada):
    d = c8.shape[1]
    n = w_ada.shape[1]
    return pl.pallas_call(
        _mod_kernel,
        grid=(n // d,),
        in_specs=[pl.BlockSpec((8, d), lambda j: (0, 0)),
                  pl.BlockSpec((d, d), lambda j: (0, j)),
                  pl.BlockSpec((1, d), lambda j: (0, j))],
        out_specs=pl.BlockSpec((8, d), lambda j: (0, j)),
        out_shape=jax.ShapeDtypeStruct((8, n), F32),
        compiler_params=_params("parallel"),
        name="mod",
    )(c8, w_ada, b_ada)


def _inproj_kernel(x_ref, g_ref, sc_ref, sh_ref, w_ref, ws_ref, o_ref, s_ref, u_sc):
    @pl.when(pl.program_id(1) == 0)
    def _():
        xf = x_ref[...]
        ms = jnp.mean(xf * xf, axis=-1, keepdims=True)
        y = xf * lax.rsqrt(ms + RMS_EPS) * g_ref[...]
        u = (y * (1.0 + sc_ref[0]) + sh_ref[0]).astype(BF16)
        u_sc[...] = u
        s_ref[...] = _dot(u, ws_ref[...])

    o_ref[...] = _dot(u_sc[...], w_ref[...]).astype(BF16)


def _inproj_call(x2, g, scale, shift, w_main, w_small, T, tm=512, tn=1920):
    n, d = x2.shape
    tpb = T // tm
    return pl.pallas_call(
        _inproj_kernel,
        grid=(n // tm, MAIN_COLS // tn),
        in_specs=[pl.BlockSpec((tm, d), lambda i, j: (i, 0)),
                  pl.BlockSpec((1, d), lambda i, j: (0, 0)),
                  pl.BlockSpec((1, 1, d), lambda i, j: (i // tpb, 0, 0)),
                  pl.BlockSpec((1, 1, d), lambda i, j: (i // tpb, 0, 0)),
                  pl.BlockSpec((d, tn), lambda i, j: (0, j)),
                  pl.BlockSpec((d, LANES), lambda i, j: (0, 0))],
        out_specs=[pl.BlockSpec((tm, tn), lambda i, j: (i, j)),
                   pl.BlockSpec((tm, LANES), lambda i, j: (i, 0))],
        out_shape=[jax.ShapeDtypeStruct((n, MAIN_COLS), BF16),
                   jax.ShapeDtypeStruct((n, LANES), F32)],
        scratch_shapes=[pltpu.VMEM((tm, d), BF16)],
        compiler_params=_params("parallel", "arbitrary"),
        name="inproj",
    )(x2, g, scale, shift, w_main, w_small)


def _gla_kernel(q_ref, k_ref, v_ref, r_ref, s_ref, wa_ref, ba_ref, g_ref, o_ref, st_sc, *, tt):
    @pl.when(pl.program_id(2) == 0)
    def _():
        st_sc[...] = jnp.zeros_like(st_sc)

    C = GLA_CHUNK
    ga = s_ref[:, S_GA:S_GA + GLA_GATE_RANK]
    z = _dot3(ga, wa_ref[...]) + ba_ref[...]
    la = (jnp.minimum(z, 0.0) - jnp.log(1.0 + jnp.exp(-jnp.abs(z)))) * (1.0 / GLA_TAU)
    row = lax.broadcasted_iota(jnp.int32, (C, C), 0)
    col = lax.broadcasted_iota(jnp.int32, (C, C), 1)
    causal = col <= row
    tril = jnp.where(causal, 1.0, 0.0).astype(BF16)
    st = st_sc[...]
    for c in range(tt // C):
        sl = slice(c * C, (c + 1) * C)
        hi, lo = _split(la[sl])
        bc = _dot(tril, hi) + _dot(tril, lo)
        bl = bc[C - 1:C, :]
        qc = q_ref[sl, :].astype(F32) * (GLA_DK ** -0.5)
        kc = k_ref[sl, :].astype(F32)
        vc = v_ref[sl, :]
        qe = (qc * jnp.exp(bc)).astype(BF16)
        ke = (kc * jnp.exp(-bc)).astype(BF16)
        kd = (kc * jnp.exp(bl - bc)).astype(BF16)
        a = jnp.where(causal, _dot_nt(qe, ke), 0.0)
        o = _dot(a.astype(BF16), vc) + _dot_nt(qe, st.astype(BF16))
        st = st * jnp.exp(bl) + _dot_tn(vc, kd)
        y = o * lax.rsqrt(jnp.mean(o * o, axis=-1, keepdims=True) + RMS_EPS) * g_ref[...]
        r = r_ref[sl, :].astype(F32)
        o_ref[sl, :] = (y * (r * _sigmoid(r))).astype(BF16)
    st_sc[...] = st


def _gla_call(main, small, wa2, ba, norm_g, B, T, tt=256):
    n = main.shape[0]
    nt = T // tt
    rowmap = lambda b, h, t: b * nt + t
    return pl.pallas_call(
        functools.partial(_gla_kernel, tt=tt),
        grid=(B, GLA_HEADS, nt),
        in_specs=[pl.BlockSpec((tt, GLA_DK), lambda b, h, t: (rowmap(b, h, t), C_GQ // GLA_DK + h)),
                  pl.BlockSpec((tt, GLA_DK), lambda b, h, t: (rowmap(b, h, t), C_GK // GLA_DK + h)),
                  pl.BlockSpec((tt, GLA_DV), lambda b, h, t: (rowmap(b, h, t), C_GV // GLA_DV + h)),
                  pl.BlockSpec((tt, GLA_DV), lambda b, h, t: (rowmap(b, h, t), C_GR // GLA_DV + h)),
                  pl.BlockSpec((tt, LANES), lambda b, h, t: (rowmap(b, h, t), 0)),
                  pl.BlockSpec((GLA_GATE_RANK, GLA_DK), lambda b, h, t: (0, h)),
                  pl.BlockSpec((1, GLA_DK), lambda b, h, t: (0, h)),
                  pl.BlockSpec((1, GLA_DV), lambda b, h, t: (0, 0))],
        out_specs=pl.BlockSpec((tt, GLA_DV), lambda b, h, t: (rowmap(b, h, t), h)),
        out_shape=jax.ShapeDtypeStruct((n, GLA_HEADS * GLA_DV), BF16),
        scratch_shapes=[pltpu.VMEM((GLA_DV, GLA_DK), F32)],
        compiler_params=_params("parallel", "parallel", "arbitrary"),
        name="gla",
    )(main, main, main, main, small, wa2, ba, norm_g)


def _compress_kernel(x_ref, pe_ref, w1_ref, w2_ref, o_ref):
    half = CMP_STRIDE * NSA_HD
    x = x_ref[0, 0]
    nb = x.shape[0]
    ylo = _dot(x, w1_ref[0, :half, :])
    yhi = _dot(x, w1_ref[0, half:, :])
    pe = jnp.broadcast_to(pe_ref[0], (8, 2 * half)).astype(BF16)
    pterm = _dot(pe, w1_ref[0])[0:1, :]
    pre = ylo + pltpu.roll(yhi, nb - 1, 0) + pterm
    hcu = pre * pre * pre
    hid = 0.5 * pre * (1.0 + jnp.tanh(0.7978845608028654 * (pre + 0.044715 * hcu)))
    o_ref[0, 0] = _dot(hid.astype(BF16), w2_ref[0]).astype(BF16)


def _compress_call(xkv, pe, w1, w2):
    _, bg, nb, wdt = xkv.shape
    return pl.pallas_call(
        _compress_kernel,
        grid=(2, bg),
        in_specs=[pl.BlockSpec((1, 1, nb, wdt), lambda s, i: (s, i, 0, 0)),
                  pl.BlockSpec((1, 1, CMP_LEN * NSA_HD), lambda s, i: (s, 0, 0)),
                  pl.BlockSpec((1, CMP_LEN * NSA_HD, CMP_HIDDEN), lambda s, i: (s, 0, 0)),
                  pl.BlockSpec((1, CMP_HIDDEN, NSA_HD), lambda s, i: (s, 0, 0))],
        out_specs=pl.BlockSpec((1, 1, nb, NSA_HD), lambda s, i: (s, i, 0, 0)),
        out_shape=jax.ShapeDtypeStruct((2, bg, nb, NSA_HD), BF16),
        compiler_params=_params("parallel", "parallel"),
        name="compress",
    )(xkv, pe, w1, w2)


def _cmpsel_kernel(q_ref, kc_ref, vc_ref, o_ref, sel_ref, *, tq, nsel):
    t0 = pl.program_id(2) * tq
    kc = kc_ref[0, 0]
    vc = vc_ref[0, 0]
    ncp = kc.shape[0]
    tpos = t0 + lax.broadcasted_iota(jnp.int32, (tq, 1), 0)
    cidx = lax.broadcasted_iota(jnp.int32, (1, ncp), 1)
    cmask = (cidx * CMP_STRIDE + (CMP_LEN - 1)) <= tpos
    psum = jnp.zeros((tq, ncp), F32)
    for r in range(NSA_GROUP):
        q = q_ref[:, r * NSA_HD:(r + 1) * NSA_HD]
        s = jnp.where(cmask, _dot_nt(q, kc) * (NSA_HD ** -0.5), NEG)
        m = jnp.max(s, axis=-1, keepdims=True)
        e = jnp.where(cmask, jnp.exp(s - m), 0.0)
        p = e / jnp.maximum(jnp.sum(e, axis=-1, keepdims=True), 1e-30)
        o_ref[:, r * NSA_HD:(r + 1) * NSA_HD] = _dot(p.astype(BF16), vc).astype(BF16)
        psum = psum + p
    ci = lax.broadcasted_iota(jnp.int32, (nsel, ncp), 1) * CMP_STRIDE
    si = lax.broadcasted_iota(jnp.int32, (nsel, ncp), 0) * SEL_BLOCK
    ov = jnp.where((ci < si + SEL_BLOCK) & (ci + CMP_LEN > si), 1.0, 0.0).astype(BF16)
    hi, lo = _split(psum)
    imp = _dot_nt(ov, hi) + _dot_nt(ov, lo)
    tlane = t0 + lax.broadcasted_iota(jnp.int32, (1, tq), 1)
    jblk = lax.broadcasted_iota(jnp.int32, (nsel, 1), 0)
    tblk = tlane // SEL_BLOCK
    forced = (jblk == 0) | (jblk == tblk) | (jblk == tblk - 1)
    valid = (jblk * SEL_BLOCK) <= tlane
    score = jnp.where(forced, jnp.inf, jnp.where(valid, imp, -jnp.inf))
    rank = jnp.zeros((nsel, tq), F32)
    for i in range(nsel):
        si_ = score[i:i + 1, :]
        tie = jnp.where(jblk > i, 1.0, 0.0)
        rank = rank + jnp.where(si_ > score, 1.0, jnp.where(si_ == score, tie, 0.0))
    ntop = min(SEL_TOPK, nsel)
    chosen = jnp.where(valid & (rank < ntop), 1.0, 0.0)
    sel_ref[0, 0] = jnp.transpose(chosen).astype(BF16)


def _cmpsel_call(main, kvc, B, T, tq=256):
    n = main.shape[0]
    nq = T // tq
    nsel = T // SEL_BLOCK
    ncp = kvc.shape[2]
    G = NSA_KV_HEADS
    qw = NSA_GROUP * NSA_HD
    return pl.pallas_call(
        functools.partial(_cmpsel_kernel, tq=tq, nsel=nsel),
        grid=(B, G, nq),
        in_specs=[pl.BlockSpec((tq, qw), lambda b, g, t: (b * nq + t, C_NQ // qw + g)),
                  pl.BlockSpec((1, 1, ncp, NSA_HD), lambda b, g, t: (0, b * G + g, 0, 0)),
                  pl.BlockSpec((1, 1, ncp, NSA_HD), lambda b, g, t: (1, b * G + g, 0, 0))],
        out_specs=[pl.BlockSpec((tq, qw), lambda b, g, t: (b * nq + t, g)),
                   pl.BlockSpec((1, 1, tq, nsel), lambda b, g, t: (b, g, t, 0))],
        out_shape=[jax.ShapeDtypeStruct((n, NSA_HEADS * NSA_HD), BF16),
                   jax.ShapeDtypeStruct((B, G, T, nsel), BF16)],
        compiler_params=_params("parallel", "parallel", "parallel"),
        name="cmpsel",
    )(main, kvc, kvc)


def _nsa_kernel(q_ref, ks_ref, vs_ref, kw_ref, vw_ref, sel_ref, oc_ref, s_ref, o_ref, s_sc, *, tq, tk, T, nsel):
    R = NSA_GROUP
    hd = NSA_HD
    g = pl.program_id(1)
    t0 = pl.program_id(2) * tq
    qscale = (hd ** -0.5) * LOG2E
    q4 = jnp.concatenate([q_ref[:, r * hd:(r + 1) * hd] for r in range(R)], axis=0)
    q4 = (q4.astype(F32) * qscale).astype(BF16)
    rowpos = t0 + lax.broadcasted_iota(jnp.int32, (tq, 1), 0)
    selm = sel_ref[0, 0].astype(F32) - 1.0
    selm4 = jnp.concatenate([selm] * R, axis=0).astype(BF16)

    qa = jnp.concatenate([q4, selm4], axis=1)
    jcol = lax.broadcasted_iota(jnp.int32, (tk, nsel), 1)
    crow = lax.broadcasted_iota(jnp.int32, (tk, nsel), 0) // SEL_BLOCK
    kcol = lax.broadcasted_iota(jnp.int32, (1, tk), 1)

    def scores(kt):
        k0 = pl.multiple_of(kt * tk, tk)
        ind = jnp.where(jcol == crow + kt * (tk // SEL_BLOCK), 1e30, 0.0).astype(BF16)
        s_sc[kt % 2] = _dot_nt(qa, jnp.concatenate([ks_ref[pl.ds(k0, tk), :], ind], axis=1))

    def attend(kt, carry, diagonal):
        m, l, acc = carry
        k0 = pl.multiple_of(kt * tk, tk)
        s = s_sc[kt % 2].reshape(R, tq, tk)
        if diagonal:
            s = s + jnp.where((k0 + kcol) <= rowpos, 0.0, NEG)[None]
        m_new = jnp.maximum(m, jnp.max(s, axis=-1, keepdims=True))
        alpha = jnp.exp2(m - m_new)
        p = jnp.exp2(s - m_new)
        l = alpha * l + jnp.sum(p, axis=-1, keepdims=True)
        pv = _dot(p.reshape(R * tq, tk).astype(BF16), vs_ref[pl.ds(k0, tk), :]).reshape(R, tq, hd)
        return m_new, l, alpha * acc + pv

    def step(kt, carry):
        carry = attend(kt, carry, False)
        scores(kt + 1)
        return carry

    init = (jnp.full((R, tq, 1), NEG, F32), jnp.zeros((R, tq, 1), F32), jnp.zeros((R, tq, hd), F32))
    kdiag = t0 // tk
    scores(0)
    carry = lax.fori_loop(0, kdiag, step, init)
    _, l_s, acc_s = attend(kdiag, carry, True)
    o_sel = acc_s / l_s

    wlen = WIN + tq
    w0 = pl.multiple_of(jnp.maximum(t0 - WIN, 0), tq)
    kw = kw_ref[pl.ds(w0, wlen), :]
    vw = vw_ref[pl.ds(w0, wlen), :]
    kpos = w0 + lax.broadcasted_iota(jnp.int32, (1, wlen), 1)
    wbias = jnp.where((kpos <= rowpos) & (rowpos - kpos < WIN), 0.0, NEG)
    sw = _dot_nt(q4, kw).reshape(R, tq, wlen) + wbias[None]
    mw = jnp.max(sw, axis=-1, keepdims=True)
    pw = jnp.exp2(sw - mw)
    lw = jnp.sum(pw, axis=-1, keepdims=True)
    o_win = _dot(pw.reshape(R * tq, wlen).astype(BF16), vw).reshape(R, tq, hd) / lw

    ng = R * 3
    graw = jnp.where(g == 0, s_ref[:, S_NG:S_NG + ng], s_ref[:, S_NG + ng:S_NG + 2 * ng])
    gates = _sigmoid(graw)
    for r in range(R):
        oc = oc_ref[:, r * hd:(r + 1) * hd].astype(F32)
        out = (gates[:, 3 * r:3 * r + 1] * oc + gates[:, 3 * r + 1:3 * r + 2] * o_sel[r]
               + gates[:, 3 * r + 2:3 * r + 3] * o_win[r])
        o_ref[:, r * hd:(r + 1) * hd] = out.astype(BF16)


def _nsa_call(main, small, sel, o_cmp, B, T, tq=128, tk=512):
    n = main.shape[0]
    nq = T // tq
    nsel = T // SEL_BLOCK
    G = NSA_KV_HEADS
    hd = NSA_HD
    qw = NSA_GROUP * hd
    tk = min(tk, T)
    kvb = C_KV // hd

    def kvspec(slot):
        return pl.BlockSpec((T, hd), lambda b, g, t: (b, kvb + 2 * slot + g))

    return pl.pallas_call(
        functools.partial(_nsa_kernel, tq=tq, tk=tk, T=T, nsel=nsel),
        grid=(B, G, nq),
        in_specs=[pl.BlockSpec((tq, qw), lambda b, g, t: (b * nq + t, C_NQ // qw + g)),
                  kvspec(2), kvspec(3), kvspec(4), kvspec(5),
                  pl.BlockSpec((1, 1, tq, nsel), lambda b, g, t: (b, g, t, 0)),
                  pl.BlockSpec((tq, qw), lambda b, g, t: (b * nq + t, g)),
                  pl.BlockSpec((tq, LANES), lambda b, g, t: (b * nq + t, 0))],
        out_specs=pl.BlockSpec((tq, qw), lambda b, g, t: (b * nq + t, g)),
        out_shape=jax.ShapeDtypeStruct((n, NSA_HEADS * hd), BF16),
        scratch_shapes=[pltpu.VMEM((2, NSA_GROUP * tq, tk), F32)],
        compiler_params=_params("parallel", "parallel", "parallel"),
        name="nsa",
    )(main, main, main, main, main, sel, o_cmp, small)


def _merge_kernel(yg_ref, yn_ref, mg_ref, x_ref, g1_ref, sc_ref, sh_ref, n2_ref, pg_ref, pn_ref, wo_ref,
                  rw_ref, rb_ref, h_ref, u_ref, rt_ref, cnt_ref, carry_sc, *, tm):
    d = D_MODEL

    @pl.when(pl.program_id(0) == 0)
    def _():
        carry_sc[...] = jnp.zeros_like(carry_sc)

    a = _dot(yg_ref[...], pg_ref[...])
    b = _dot(yn_ref[...], pn_ref[...])
    mg = _sigmoid(mg_ref[...].astype(F32))
    merged = mg[:, :d] * a + mg[:, d:] * b
    mix = _dot(merged.astype(BF16), wo_ref[...])
    h = x_ref[...] + g1_ref[0] * mix
    h_ref[...] = h
    y = h * lax.rsqrt(jnp.mean(h * h, axis=-1, keepdims=True) + RMS_EPS) * n2_ref[...]
    u = y * (1.0 + sc_ref[0]) + sh_ref[0]
    u_ref[...] = u

    lane = lax.broadcasted_iota(jnp.int32, (1, LANES), 1).astype(F32)
    logits = jnp.where(lane < N_EXPERTS, _dot3(u, rw_ref[...]) + rb_ref[...], -jnp.inf)
    vals, idxs = [], []
    cur = logits
    for _ in range(TOP_K):
        m = jnp.max(cur, axis=-1, keepdims=True)
        idx = jnp.min(jnp.where(cur == m, lane, float(LANES)), axis=-1, keepdims=True)
        vals.append(m)
        idxs.append(idx)
        cur = jnp.where(lane == idx, -jnp.inf, cur)
    es = [jnp.exp(v - vals[0]) for v in vals]
    den = es[0] + es[1] + es[2] + es[3]
    onehots = [lane == idx for idx in idxs]
    oh = jnp.zeros((tm, LANES), F32)
    for o in onehots:
        oh = oh + jnp.where(o, 1.0, 0.0)
    row = lax.broadcasted_iota(jnp.int32, (tm, tm), 0)
    col = lax.broadcasted_iota(jnp.int32, (tm, tm), 1)
    tril = jnp.where(col <= row, 1.0, 0.0).astype(BF16)
    cnt = _dot(tril, oh.astype(BF16))
    tot = cnt + carry_sc[0:1, :]
    route = jnp.zeros((tm, LANES), F32)
    for k in range(TOP_K):
        pos = jnp.sum(jnp.where(onehots[k], tot - 1.0, 0.0), axis=-1, keepdims=True)
        route = jnp.where(lane == k, es[k] / den, route)
        route = jnp.where(lane == TOP_K + k, idxs[k], route)
        route = jnp.where(lane == 2 * TOP_K + k, pos, route)
    rt_ref[...] = route
    newc = tot[tm - 1:tm, :]
    carry_sc[...] = jnp.broadcast_to(newc, carry_sc.shape)
    cnt_ref[...] = jnp.broadcast_to(newc, cnt_ref.shape)


def _merge_call(y_gla, y_nsa, main, x2, gate1, scale2, shift2, n2g, pg, pn, wo, rw, rb, T, tm=512):
    n, d = x2.shape
    tpb = T // tm
    bmap = lambda i: (i // tpb, 0, 0)
    full = lambda i: (0, 0)
    return pl.pallas_call(
        functools.partial(_merge_kernel, tm=tm),
        grid=(n // tm,),
        in_specs=[pl.BlockSpec((tm, d), lambda i: (i, 0)),
                  pl.BlockSpec((tm, d), lambda i: (i, 0)),
                  pl.BlockSpec((tm, 2 * d), lambda i: (i, C_MG // (2 * d))),
                  pl.BlockSpec((tm, d), lambda i: (i, 0)),
                  pl.BlockSpec((1, 1, d), bmap), pl.BlockSpec((1, 1, d), bmap), pl.BlockSpec((1, 1, d), bmap),
                  pl.BlockSpec((1, d), full),
                  pl.BlockSpec((d, d), full), pl.BlockSpec((d, d), full), pl.BlockSpec((d, d), full),
                  pl.BlockSpec((d, LANES), full), pl.BlockSpec((1, LANES), full)],
        out_specs=[pl.BlockSpec((tm, d), lambda i: (i, 0)),
                   pl.BlockSpec((tm, d), lambda i: (i, 0)),
                   pl.BlockSpec((tm, LANES), lambda i: (i, 0)),
                   pl.BlockSpec((8, LANES), full)],
        out_shape=[jax.ShapeDtypeStruct((n, d), F32), jax.ShapeDtypeStruct((n, d), F32),
                   jax.ShapeDtypeStruct((n, LANES), F32), jax.ShapeDtypeStruct((8, LANES), F32)],
        scratch_shapes=[pltpu.VMEM((8, LANES), F32)],
        compiler_params=_params("arbitrary"),
        name="merge",
    )(y_gla, y_nsa, main, x2, gate1, scale2, shift2, n2g, pg, pn, wo, rw, rb)


def _dispatch_kernel(dest_ref, u_ref, xz_ref, xs_ref, sem, *, td):
    del xz_ref
    base = pl.program_id(0) * (td * TOP_K)

    def row_copy(r, slot):
        return pltpu.make_async_copy(u_ref.at[pl.ds(r, 1)], xs_ref.at[pl.ds(slot, 1)], sem)

    def issue(r, c):
        for k in range(TOP_K):
            row_copy(r, dest_ref[base + r * TOP_K + k]).start()
        return c

    def drain(r, c):
        for k in range(TOP_K):
            row_copy(r, dest_ref[base + r * TOP_K + k]).wait()
        return c

    lax.fori_loop(0, td, issue, 0)
    lax.fori_loop(0, td, drain, 0)


def _dispatch_call(dest, u2, xs_zero, td=256):
    n, d = u2.shape
    return pl.pallas_call(
        functools.partial(_dispatch_kernel, td=td),
        grid_spec=pltpu.PrefetchScalarGridSpec(
            num_scalar_prefetch=1,
            grid=(n // td,),
            in_specs=[pl.BlockSpec((td, d), lambda i, dest: (i, 0)),
                      pl.BlockSpec(memory_space=pl.ANY)],
            out_specs=pl.BlockSpec(memory_space=pl.ANY),
            scratch_shapes=[pltpu.SemaphoreType.DMA(())]),
        out_shape=jax.ShapeDtypeStruct(xs_zero.shape, xs_zero.dtype),
        input_output_aliases={2: 0},
        compiler_params=_params("arbitrary"),
        name="dispatch",
    )(dest, u2, xs_zero)


def _moe_kernel(be_ref, na_ref, x_ref, w1_ref, b1_ref, w2_ref, b2_ref, y_ref, w1b, w2b):
    i = pl.program_id(0)
    e = be_ref[i]
    prev = be_ref[jnp.maximum(i - 1, 0)]
    active = i < na_ref[0]
    f = D_EXPERT

    @pl.when(active & ((i == 0) | (e != prev)))
    def _():
        w1b[...] = w1_ref[0].astype(BF16)
        w2b[...] = w2_ref[0].astype(BF16)

    @pl.when(active)
    def _():
        hb = _dot(x_ref[...].astype(BF16), w1b[...]) + b1_ref[0]
        gate = jnp.minimum(hb[:, :f], SWIGLU_LIMIT)
        up = jnp.clip(hb[:, f:], -SWIGLU_LIMIT, SWIGLU_LIMIT)
        act = gate * _sigmoid(SWIGLU_ALPHA * gate) * (up + 1.0)
        y_ref[...] = _dot(act.astype(BF16), w2b[...]) + b2_ref[0]

    @pl.when(jnp.logical_not(active))
    def _():
        y_ref[...] = jnp.zeros_like(y_ref)


def _moe_call(blk_e, nact, xs, w1, b1, w2, b2):
    p, d = xs.shape
    f2 = w1.shape[2]
    m = MOE_ROWS
    return pl.pallas_call(
        _moe_kernel,
        grid_spec=pltpu.PrefetchScalarGridSpec(
            num_scalar_prefetch=2,
            grid=(p // m,),
            in_specs=[pl.BlockSpec((m, d), lambda i, be, na: (i, 0)),
                      pl.BlockSpec((1, d, f2), lambda i, be, na: (be[i], 0, 0)),
                      pl.BlockSpec((1, 1, f2), lambda i, be, na: (be[i], 0, 0)),
                      pl.BlockSpec((1, f2 // 2, d), lambda i, be, na: (be[i], 0, 0)),
                      pl.BlockSpec((1, 1, d), lambda i, be, na: (be[i], 0, 0))],
            out_specs=pl.BlockSpec((m, d), lambda i, be, na: (i, 0)),
            scratch_shapes=[pltpu.VMEM((d, f2), BF16), pltpu.VMEM((f2 // 2, d), BF16)]),
        out_shape=jax.ShapeDtypeStruct((p, d), F32),
        compiler_params=_params("arbitrary"),
        name="moe",
    )(blk_e, nact, xs, w1, b1, w2, b2)


def _combine_kernel(dest_ref, h_ref, rt_ref, g2_ref, fg_ref, ys_ref, o_ref, buf, sem, *, tc):
    base = pl.program_id(0) * (tc * TOP_K)

    def row_copy(r, k, slot):
        return pltpu.make_async_copy(ys_ref.at[pl.ds(slot, 1)], buf.at[k, pl.ds(r, 1)], sem)

    def issue(r, c):
        for k in range(TOP_K):
            row_copy(r, k, dest_ref[base + r * TOP_K + k]).start()
        return c

    def drain(r, c):
        for k in range(TOP_K):
            row_copy(r, k, dest_ref[base + r * TOP_K + k]).wait()
        return c

    lax.fori_loop(0, tc, issue, 0)
    lax.fori_loop(0, tc, drain, 0)
    rt = rt_ref[...]
    moe = rt[:, 0:1] * buf[0]
    for k in range(1, TOP_K):
        moe = moe + rt[:, k:k + 1] * buf[k]
    hh = h_ref[...] + g2_ref[0] * moe
    o_ref[...] = hh * lax.rsqrt(jnp.mean(hh * hh, axis=-1, keepdims=True) + RMS_EPS) * fg_ref[...]


def _combine_call(dest, h, route, gate2, fg, ys, T, tc=128):
    n, d = h.shape
    tpb = T // tc
    return pl.pallas_call(
        functools.partial(_combine_kernel, tc=tc),
        grid_spec=pltpu.PrefetchScalarGridSpec(
            num_scalar_prefetch=1,
            grid=(n // tc,),
            in_specs=[pl.BlockSpec((tc, d), lambda i, dest: (i, 0)),
                      pl.BlockSpec((tc, LANES), lambda i, dest: (i, 0)),
                      pl.BlockSpec((1, 1, d), lambda i, dest: (i // tpb, 0, 0)),
                      pl.BlockSpec((1, d), lambda i, dest: (0, 0)),
                      pl.BlockSpec(memory_space=pl.ANY)],
            out_specs=pl.BlockSpec((tc, d), lambda i, dest: (i, 0)),
            scratch_shapes=[pltpu.VMEM((TOP_K, tc, d), F32), pltpu.SemaphoreType.DMA(())]),
        out_shape=jax.ShapeDtypeStruct((n, d), F32),
        compiler_params=_params("arbitrary"),
        name="combine",
    )(dest, h, route, gate2, fg, ys)


def _mixer(x2, mod, norm1_g, w_in, gla_wa2, gla_ba, gla_norm_g, cmp_pe, cmp_w1, cmp_w2, B, T):
    d = D_MODEL
    G = NSA_KV_HEADS
    hd = NSA_HD
    shift1 = mod[:, 0 * d:1 * d].reshape(B, 1, d)
    scale1 = mod[:, 1 * d:2 * d].reshape(B, 1, d)
    w_main = jnp.concatenate([w_in[:, :_O_GA], w_in[:, _O_NQ:_O_KV], w_in[:, _O_MG:_O_END],
                              w_in[:, _O_KV:_O_NG]], axis=1).astype(BF16)
    w_small = jnp.concatenate([w_in[:, _O_GA:_O_NQ], w_in[:, _O_NG:_O_MG],
                               jnp.zeros((d, LANES - GLA_GATE_RANK - NSA_HEADS * 3), F32)], axis=1).astype(BF16)
    main, small = _inproj_call(x2, norm1_g.reshape(1, d), scale1, shift1, w_main, w_small, T)
    y_gla = _gla_call(main, small, gla_wa2, gla_ba.reshape(1, -1), gla_norm_g.reshape(1, -1), B, T)
    kv = main[:, C_KV:C_KV + 2 * G * hd].reshape(B, T // CMP_STRIDE, CMP_STRIDE, 2, G, hd)
    xkv = kv.transpose(3, 0, 4, 1, 2, 5).reshape(2, B * G, T // CMP_STRIDE, CMP_STRIDE * hd)
    kvc = _compress_call(xkv, cmp_pe.reshape(2, 1, CMP_LEN * hd), cmp_w1.astype(BF16), cmp_w2.astype(BF16))
    o_cmp, sel = _cmpsel_call(main, kvc, B, T)
    y_nsa = _nsa_call(main, small, sel, o_cmp, B, T)
    return main, y_gla, y_nsa


def _route_tables(route, counts, n):
    m = MOE_ROWS
    e = route[:, TOP_K:2 * TOP_K].astype(jnp.int32)
    pos = route[:, 2 * TOP_K:3 * TOP_K].astype(jnp.int32)
    cnt = counts[0, :N_EXPERTS].astype(jnp.int32)
    padded = ((cnt + m - 1) // m) * m
    pends = jnp.cumsum(padded)
    pstarts = pends - padded
    dest = (pstarts[e] + pos).reshape(-1)
    p_rows = n * TOP_K + N_EXPERTS * m
    nblk = p_rows // m
    starts = jnp.arange(nblk, dtype=jnp.int32) * m
    blk_e = jnp.minimum(jnp.sum((pends[None, :] <= starts[:, None]).astype(jnp.int32), axis=1), N_EXPERTS - 1)
    nact = (pends[-1:] // m).astype(jnp.int32)
    return dest, blk_e, nact, p_rows


def kernel(x, c, w_ada, b_ada, norm1_g, w_in, gla_wa2, gla_ba, gla_norm_g, cmp_pe, cmp_w1, cmp_w2, proj_gla,
           proj_nsa, w_out, norm2_g, router_w, router_b, moe_w1, moe_b1, moe_w2, moe_b2, final_g):
    B, T, d = x.shape
    n = B * T
    assert w_ada.shape[0] == 1 and d == D_MODEL and B <= 8, "single-layer block of width D_MODEL"
    l = 0
    h = x.reshape(n, d)
    c8 = jnp.concatenate([c, jnp.zeros((8 - B, d), c.dtype)], axis=0)
    mod = _mod_call(c8, w_ada[l], b_ada[l].reshape(1, -1))[:B]
    gate1 = mod[:, 2 * d:3 * d].reshape(B, 1, d)
    shift2 = mod[:, 3 * d:4 * d].reshape(B, 1, d)
    scale2 = mod[:, 4 * d:5 * d].reshape(B, 1, d)
    gate2 = mod[:, 5 * d:6 * d].reshape(B, 1, d)
    main, y_gla, y_nsa = _mixer(h, mod, norm1_g[l], w_in[l], gla_wa2[l], gla_ba[l], gla_norm_g[l],
                                cmp_pe[l], cmp_w1[l], cmp_w2[l], B, T)
    rw = jnp.concatenate([router_w[l], jnp.zeros((d, LANES - N_EXPERTS), F32)], axis=1)
    rb = jnp.concatenate([router_b[l], jnp.zeros((LANES - N_EXPERTS,), F32)]).reshape(1, LANES)
    h1, u2, route, counts = _merge_call(
        y_gla, y_nsa, main, h, gate1, scale2, shift2, norm2_g[l].reshape(1, d),
        proj_gla[l].astype(BF16), proj_nsa[l].astype(BF16), w_out[l].astype(BF16), rw, rb, T)
    dest, blk_e, nact, p_rows = _route_tables(route, counts, n)
    xs = _dispatch_call(dest, u2, jnp.zeros((p_rows, d), F32))
    ys = _moe_call(blk_e, nact, xs, moe_w1[l], moe_b1[l].reshape(N_EXPERTS, 1, -1), moe_w2[l],
                   moe_b2[l].reshape(N_EXPERTS, 1, -1))
    out = _combine_call(dest, h1, route, gate2, final_g.reshape(1, d), ys, T)
    return out.reshape(B, T, d)
```

```python
import functools

import jax
import jax.numpy as jnp
from jax import lax
from jax.experimental import pallas as pl
from jax.experimental.pallas import tpu as pltpu

F32 = jnp.float32
BF16 = jnp.bfloat16

D_MODEL = 1024
GLA_HEADS = 4
GLA_DK = 128
GLA_DV = 256
GLA_GATE_RANK = 16
GLA_TAU = 16.0
GLA_CHUNK = 64
NSA_HEADS = 8
NSA_KV_HEADS = 2
NSA_HD = 128
NSA_GROUP = NSA_HEADS // NSA_KV_HEADS
CMP_LEN = 32
CMP_STRIDE = 16
CMP_HIDDEN = 2 * NSA_HD
SEL_BLOCK = 64
SEL_TOPK = 16
WIN = 512
N_EXPERTS = 32
TOP_K = 4
D_EXPERT = D_MODEL
SWIGLU_LIMIT = 7.0
SWIGLU_ALPHA = 1.702
RMS_EPS = 1e-5

LANES = 128
NEG = -1e30
LOG2E = 1.4426950408889634
MOE_ROWS = 512
VMEM_LIMIT = 56 * 1024 * 1024

C_GQ, C_GK, C_GV, C_GR, C_NQ, C_MG, C_KV = 0, 512, 1024, 2048, 3072, 4096, 6144
MAIN_COLS = 7680
S_GA, S_NG = 0, 16

_O_GA = 3072
_O_NQ = 3088
_O_KV = 4112
_O_NG = 5648
_O_MG = 5672
_O_END = 7720


def _dot(a, b):
    return jnp.dot(a, b, preferred_element_type=F32)


def _dot_nt(a, b):
    return lax.dot_general(a, b, (((1,), (1,)), ((), ())), preferred_element_type=F32)


def _dot_tn(a, b):
    return lax.dot_general(a, b, (((0,), (0,)), ((), ())), preferred_element_type=F32)


def _split(a):
    hi = a.astype(BF16)
    lo = (a - hi.astype(F32)).astype(BF16)
    return hi, lo


def _dot3(a, b):
    ah, al = _split(a)
    bh, bl = _split(b)
    return _dot(ah, bh) + _dot(ah, bl) + _dot(al, bh)


def _sigmoid(x):
    return 1.0 / (1.0 + jnp.exp(-x))


def _params(*sem):
    return pltpu.CompilerParams(dimension_semantics=sem, vmem_limit_bytes=VMEM_LIMIT)


def _mod_kernel(c_ref, w_ref, b_ref, o_ref):
    c = c_ref[...]
    cond = c * _sigmoid(c)
    o_ref[...] = _dot3(cond, w_ref[...]) + b_ref[...]


def _mod_call(c8, w_ada, b_ada):
    d = c8.shape[1]
    n = w_ada.shape[1]
    return pl.pallas_call(
        _mod_kernel,
        grid=(n // d,),
        in_specs=[pl.BlockSpec((8, d), lambda j: (0, 0)),
                  pl.BlockSpec((d, d), lambda j: (0, j)),
                  pl.BlockSpec((1, d), lambda j: (0, j))],
        out_specs=pl.BlockSpec((8, d), lambda j: (0, j)),
        out_shape=jax.ShapeDtypeStruct((8, n), F32),
        compiler_params=_params("parallel"),
        name="mod",
    )(c8, w_ada, b_ada)


def _inproj_kernel(x_ref, g_ref, sc_ref, sh_ref, w_ref, ws_ref, o_ref, s_ref, u_sc):
    @pl.when(pl.program_id(1) == 0)
    def _():
        xf = x_ref[...]
        ms = jnp.mean(xf * xf, axis=-1, keepdims=True)
        y = xf * lax.rsqrt(ms + RMS_EPS) * g_ref[...]
        u = (y * (1.0 + sc_ref[0]) + sh_ref[0]).astype(BF16)
        u_sc[...] = u
        s_ref[...] = _dot(u, ws_ref[...])

    o_ref[...] = _dot(u_sc[...], w_ref[...]).astype(BF16)


def _inproj_call(x2, g, scale, shift, w_main, w_small, T, tm=512, tn=1920):
    n, d = x2.shape
    tpb = T // tm
    return pl.pallas_call(
        _inproj_kernel,
        grid=(n // tm, MAIN_COLS // tn),
        in_specs=[pl.BlockSpec((tm, d), lambda i, j: (i, 0)),
                  pl.BlockSpec((1, d), lambda i, j: (0, 0)),
                  pl.BlockSpec((1, 1, d), lambda i, j: (i // tpb, 0, 0)),
                  pl.BlockSpec((1, 1, d), lambda i, j: (i // tpb, 0, 0)),
                  pl.BlockSpec((d, tn), lambda i, j: (0, j)),
                  pl.BlockSpec((d, LANES), lambda i, j: (0, 0))],
        out_specs=[pl.BlockSpec((tm, tn), lambda i, j: (i, j)),
                   pl.BlockSpec((tm, LANES), lambda i, j: (i, 0))],
        out_shape=[jax.ShapeDtypeStruct((n, MAIN_COLS), BF16),
                   jax.ShapeDtypeStruct((n, LANES), F32)],
        scratch_shapes=[pltpu.VMEM((tm, d), BF16)],
        compiler_params=_params("parallel", "arbitrary"),
        name="inproj",
    )(x2, g, scale, shift, w_main, w_small)


def _gla_kernel(q_ref, k_ref, v_ref, r_ref, s_ref, wa_ref, ba_ref, g_ref, o_ref, st_sc, *, tt):
    @pl.when(pl.program_id(1) == 0)
    def _():
        st_sc[...] = jnp.zeros_like(st_sc)

    C = GLA_CHUNK
    nc = tt // C
    dk, dv = GLA_DK, GLA_DV
    ga = s_ref[:, S_GA:S_GA + GLA_GATE_RANK]
    z = _dot3(ga, wa_ref[...]) + ba_ref[...]
    la = (jnp.minimum(z, 0.0) - jnp.log(1.0 + jnp.exp(-jnp.abs(z)))) * (1.0 / GLA_TAU)
    row = lax.broadcasted_iota(jnp.int32, (tt, tt), 0)
    col = lax.broadcasted_iota(jnp.int32, (tt, tt), 1)
    causal = ((row // C) == (col // C)) & (col <= row)
    tril = jnp.where(causal, 1.0, 0.0).astype(BF16)
    hi, lo = _split(la)
    bc_all = _dot(tril, hi) + _dot(tril, lo)
    for h in range(GLA_HEADS):
        bc = bc_all[:, h * dk:(h + 1) * dk]
        bls = [bc[(c + 1) * C - 1:(c + 1) * C, :] for c in range(nc)]
        blr = jnp.concatenate([jnp.broadcast_to(b, (C, dk)) for b in bls], axis=0)
        q = q_ref[:, h * dk:(h + 1) * dk].astype(F32) * (dk ** -0.5)
        k = k_ref[:, h * dk:(h + 1) * dk].astype(F32)
        v = v_ref[:, h * dv:(h + 1) * dv]
        qe = (q * jnp.exp(bc)).astype(BF16)
        ke = (k * jnp.exp(-bc)).astype(BF16)
        kd = (k * jnp.exp(blr - bc)).astype(BF16)
        a = jnp.where(causal, _dot_nt(qe, ke), 0.0)
        intra = _dot(a.astype(BF16), v)
        st = st_sc[h]
        inter = []
        for c in range(nc):
            sl = slice(c * C, (c + 1) * C)
            inter.append(_dot_nt(qe[sl], st.astype(BF16)))
            st = st * jnp.exp(bls[c]) + _dot_tn(v[sl], kd[sl])
        st_sc[h] = st
        o = intra + jnp.concatenate(inter, axis=0)
        y = o * lax.rsqrt(jnp.mean(o * o, axis=-1, keepdims=True) + RMS_EPS) * g_ref[...]
        r = r_ref[:, h * dv:(h + 1) * dv].astype(F32)
        o_ref[:, h * dv:(h + 1) * dv] = (y * (r * _sigmoid(r))).astype(BF16)


def _gla_call(main, small, wa2, ba, norm_g, B, T, tt=256):
    n = main.shape[0]
    nt = T // tt
    H = GLA_HEADS
    rowmap = lambda b, t: b * nt + t
    return pl.pallas_call(
        functools.partial(_gla_kernel, tt=tt),
        grid=(B, nt),
        in_specs=[pl.BlockSpec((tt, H * GLA_DK), lambda b, t: (rowmap(b, t), C_GQ // (H * GLA_DK))),
                  pl.BlockSpec((tt, H * GLA_DK), lambda b, t: (rowmap(b, t), C_GK // (H * GLA_DK))),
                  pl.BlockSpec((tt, H * GLA_DV), lambda b, t: (rowmap(b, t), C_GV // (H * GLA_DV))),
                  pl.BlockSpec((tt, H * GLA_DV), lambda b, t: (rowmap(b, t), C_GR // (H * GLA_DV))),
                  pl.BlockSpec((tt, LANES), lambda b, t: (rowmap(b, t), 0)),
                  pl.BlockSpec((GLA_GATE_RANK, H * GLA_DK), lambda b, t: (0, 0)),
                  pl.BlockSpec((1, H * GLA_DK), lambda b, t: (0, 0)),
                  pl.BlockSpec((1, GLA_DV), lambda b, t: (0, 0))],
        out_specs=pl.BlockSpec((tt, H * GLA_DV), lambda b, t: (rowmap(b, t), 0)),
        out_shape=jax.ShapeDtypeStruct((n, H * GLA_DV), BF16),
        scratch_shapes=[pltpu.VMEM((H, GLA_DV, GLA_DK), F32)],
        compiler_params=_params("parallel", "arbitrary"),
        name="gla",
    )(main, main, main, main, small, wa2, ba, norm_g)


def _compress_kernel(x_ref, pe_ref, w1_ref, w2_ref, o_ref):
    half = CMP_STRIDE * NSA_HD
    x = x_ref[0, 0]
    nb = x.shape[0]
    ylo = _dot(x, w1_ref[0, :half, :])
    yhi = _dot(x, w1_ref[0, half:, :])
    pe = jnp.broadcast_to(pe_ref[0], (8, 2 * half)).astype(BF16)
    pterm = _dot(pe, w1_ref[0])[0:1, :]
    pre = ylo + pltpu.roll(yhi, nb - 1, 0) + pterm
    hcu = pre * pre * pre
    hid = 0.5 * pre * (1.0 + jnp.tanh(0.7978845608028654 * (pre + 0.044715 * hcu)))
    o_ref[0, 0] = _dot(hid.astype(BF16), w2_ref[0]).astype(BF16)


def _compress_call(xkv, pe, w1, w2):
    _, bg, nb, wdt = xkv.shape
    return pl.pallas_call(
        _compress_kernel,
        grid=(2, bg),
        in_specs=[pl.BlockSpec((1, 1, nb, wdt), lambda s, i: (s, i, 0, 0)),
                  pl.BlockSpec((1, 1, CMP_LEN * NSA_HD), lambda s, i: (s, 0, 0)),
                  pl.BlockSpec((1, CMP_LEN * NSA_HD, CMP_HIDDEN), lambda s, i: (s, 0, 0)),
                  pl.BlockSpec((1, CMP_HIDDEN, NSA_HD), lambda s, i: (s, 0, 0))],
        out_specs=pl.BlockSpec((1, 1, nb, NSA_HD), lambda s, i: (s, i, 0, 0)),
        out_shape=jax.ShapeDtypeStruct((2, bg, nb, NSA_HD), BF16),
        compiler_params=_params("parallel", "parallel"),
        name="compress",
    )(xkv, pe, w1, w2)


def _cmpsel_kernel(q_ref, kc_ref, vc_ref, o_ref, sel_ref, *, tq, nsel):
    t0 = pl.program_id(2) * tq
    kc = kc_ref[0, 0]
    vc = vc_ref[0, 0]
    ncp = kc.shape[0]
    tpos = t0 + lax.broadcasted_iota(jnp.int32, (tq, 1), 0)
    cidx = lax.broadcasted_iota(jnp.int32, (1, ncp), 1)
    cmask = (cidx * CMP_STRIDE + (CMP_LEN - 1)) <= tpos
    psum = jnp.zeros((tq, ncp), F32)
    for r in range(NSA_GROUP):
        q = q_ref[:, r * NSA_HD:(r + 1) * NSA_HD]
        s = jnp.where(cmask, _dot_nt(q, kc) * (NSA_HD ** -0.5), NEG)
        m = jnp.max(s, axis=-1, keepdims=True)
        e = jnp.where(cmask, jnp.exp(s - m), 0.0)
        p = e / jnp.maximum(jnp.sum(e, axis=-1, keepdims=True), 1e-30)
        o_ref[:, r * NSA_HD:(r + 1) * NSA_HD] = _dot(p.astype(BF16), vc).astype(BF16)
        psum = psum + p
    ci = lax.broadcasted_iota(jnp.int32, (nsel, ncp), 1) * CMP_STRIDE
    si = lax.broadcasted_iota(jnp.int32, (nsel, ncp), 0) * SEL_BLOCK
    ov = jnp.where((ci < si + SEL_BLOCK) & (ci + CMP_LEN > si), 1.0, 0.0).astype(BF16)
    hi, lo = _split(psum)
    imp = _dot_nt(ov, hi) + _dot_nt(ov, lo)
    tlane = t0 + lax.broadcasted_iota(jnp.int32, (1, tq), 1)
    jblk = lax.broadcasted_iota(jnp.int32, (nsel, 1), 0)
    tblk = tlane // SEL_BLOCK
    forced = (jblk == 0) | (jblk == tblk) | (jblk == tblk - 1)
    valid = (jblk * SEL_BLOCK) <= tlane
    score = jnp.where(forced, jnp.inf, jnp.where(valid, imp, -jnp.inf))
    rank = jnp.zeros((nsel, tq), F32)
    for i in range(nsel):
        si_ = score[i:i + 1, :]
        tie = jnp.where(jblk > i, 1.0, 0.0)
        rank = rank + jnp.where(si_ > score, 1.0, jnp.where(si_ == score, tie, 0.0))
    ntop = min(SEL_TOPK, nsel)
    chosen = jnp.where(valid & (rank < ntop), 1.0, 0.0)
    sel_ref[0, 0] = jnp.transpose(chosen).astype(BF16)


def _cmpsel_call(main, kvc, B, T, tq=256):
    n = main.shape[0]
    nq = T // tq
    nsel = T // SEL_BLOCK
    ncp = kvc.shape[2]
    G = NSA_KV_HEADS
    qw = NSA_GROUP * NSA_HD
    return pl.pallas_call(
        functools.partial(_cmpsel_kernel, tq=tq, nsel=nsel),
        grid=(B, G, nq),
        in_specs=[pl.BlockSpec((tq, qw), lambda b, g, t: (b * nq + t, C_NQ // qw + g)),
                  pl.BlockSpec((1, 1, ncp, NSA_HD), lambda b, g, t: (0, b * G + g, 0, 0)),
                  pl.BlockSpec((1, 1, ncp, NSA_HD), lambda b, g, t: (1, b * G + g, 0, 0))],
        out_specs=[pl.BlockSpec((tq, qw), lambda b, g, t: (b * nq + t, g)),
                   pl.BlockSpec((1, 1, tq, nsel), lambda b, g, t: (b, g, t, 0))],
        out_shape=[jax.ShapeDtypeStruct((n, NSA_HEADS * NSA_HD), BF16),
                   jax.ShapeDtypeStruct((B, G, T, nsel), BF16)],
        compiler_params=_params("parallel", "parallel", "parallel"),
        name="cmpsel",
    )(main, kvc, kvc)


def _nsa_kernel(q_ref, ks_ref, vs_ref, kw_ref, vw_ref, sel_ref, oc_ref, s_ref, o_ref, s_sc, *, tq, tk, T, nsel):
    R = NSA_GROUP
    hd = NSA_HD
    g = pl.program_id(1)
    t0 = pl.program_id(2) * tq
    qscale = (hd ** -0.5) * LOG2E
    q4 = jnp.concatenate([q_ref[:, r * hd:(r + 1) * hd] for r in range(R)], axis=0)
    q4 = (q4.astype(F32) * qscale).astype(BF16)
    rowpos = t0 + lax.broadcasted_iota(jnp.int32, (tq, 1), 0)
    selm = sel_ref[0, 0].astype(F32) - 1.0
    selm4 = jnp.concatenate([selm] * R, axis=0).astype(BF16)

    qa = jnp.concatenate([q4, selm4], axis=1)
    jcol = lax.broadcasted_iota(jnp.int32, (tk, nsel), 1)
    crow = lax.broadcasted_iota(jnp.int32, (tk, nsel), 0) // SEL_BLOCK
    kcol = lax.broadcasted_iota(jnp.int32, (1, tk), 1)

    def scores(kt):
        k0 = pl.multiple_of(kt * tk, tk)
        ind = jnp.where(jcol == crow + kt * (tk // SEL_BLOCK), 1e30, 0.0).astype(BF16)
        s_sc[kt % 2] = _dot_nt(qa, jnp.concatenate([ks_ref[pl.ds(k0, tk), :], ind], axis=1))

    def attend(kt, carry, diagonal):
        m, l, acc = carry
        k0 = pl.multiple_of(kt * tk, tk)
        s = s_sc[kt % 2].reshape(R, tq, tk)
        if diagonal:
            s = s + jnp.where((k0 + kcol) <= rowpos, 0.0, NEG)[None]
        m_new = jnp.maximum(m, jnp.max(s, axis=-1, keepdims=True))
        alpha = jnp.exp2(m - m_new)
        p = jnp.exp2(s - m_new)
        l = alpha * l + jnp.sum(p, axis=-1, keepdims=True)
        pv = _dot(p.reshape(R * tq, tk).astype(BF16), vs_ref[pl.ds(k0, tk), :]).reshape(R, tq, hd)
        return m_new, l, alpha * acc + pv

    def step(kt, carry):
        carry = attend(kt, carry, False)
        scores(kt + 1)
        return carry

    init = (jnp.full((R, tq, 1), NEG, F32), jnp.zeros((R, tq, 1), F32), jnp.zeros((R, tq, hd), F32))
    kdiag = t0 // tk
    scores(0)
    carry = lax.fori_loop(0, kdiag, step, init)
    _, l_s, acc_s = attend(kdiag, carry, True)
    o_sel = acc_s / l_s

    wlen = WIN + tq
    w0 = pl.multiple_of(jnp.maximum(t0 - WIN, 0), tq)
    kw = kw_ref[pl.ds(w0, wlen), :]
    vw = vw_ref[pl.ds(w0, wlen), :]
    kpos = w0 + lax.broadcasted_iota(jnp.int32, (1, wlen), 1)
    wbias = jnp.where((kpos <= rowpos) & (rowpos - kpos < WIN), 0.0, NEG)
    sw = _dot_nt(q4, kw).reshape(R, tq, wlen) + wbias[None]
    mw = jnp.max(sw, axis=-1, keepdims=True)
    pw = jnp.exp2(sw - mw)
    lw = jnp.sum(pw, axis=-1, keepdims=True)
    o_win = _dot(pw.reshape(R * tq, wlen).astype(BF16), vw).reshape(R, tq, hd) / lw

    ng = R * 3
    graw = jnp.where(g == 0, s_ref[:, S_NG:S_NG + ng], s_ref[:, S_NG + ng:S_NG + 2 * ng])
    gates = _sigmoid(graw)
    for r in range(R):
        oc = oc_ref[:, r * hd:(r + 1) * hd].astype(F32)
        out = (gates[:, 3 * r:3 * r + 1] * oc + gates[:, 3 * r + 1:3 * r + 2] * o_sel[r]
               + gates[:, 3 * r + 2:3 * r + 3] * o_win[r])
        o_ref[:, r * hd:(r + 1) * hd] = out.astype(BF16)


def _nsa_call(main, small, sel, o_cmp, B, T, tq=128, tk=512):
    n = main.shape[0]
    nq = T // tq
    nsel = T // SEL_BLOCK
    G = NSA_KV_HEADS
    hd = NSA_HD
    qw = NSA_GROUP * hd
    tk = min(tk, T)
    kvb = C_KV // hd

    def kvspec(slot):
        return pl.BlockSpec((T, hd), lambda b, g, t: (b, kvb + 2 * slot + g))

    return pl.pallas_call(
        functools.partial(_nsa_kernel, tq=tq, tk=tk, T=T, nsel=nsel),
        grid=(B, G, nq),
        in_specs=[pl.BlockSpec((tq, qw), lambda b, g, t: (b * nq + t, C_NQ // qw + g)),
                  kvspec(2), kvspec(3), kvspec(4), kvspec(5),
                  pl.BlockSpec((1, 1, tq, nsel), lambda b, g, t: (b, g, t, 0)),
                  pl.BlockSpec((tq, qw), lambda b, g, t: (b * nq + t, g)),
                  pl.BlockSpec((tq, LANES), lambda b, g, t: (b * nq + t, 0))],
        out_specs=pl.BlockSpec((tq, qw), lambda b, g, t: (b * nq + t, g)),
        out_shape=jax.ShapeDtypeStruct((n, NSA_HEADS * hd), BF16),
        scratch_shapes=[pltpu.VMEM((2, NSA_GROUP * tq, tk), F32)],
        compiler_params=_params("parallel", "parallel", "parallel"),
        name="nsa",
    )(main, main, main, main, main, sel, o_cmp, small)


def _merge_kernel(yg_ref, yn_ref, mg_ref, x_ref, g1_ref, sc_ref, sh_ref, n2_ref, pg_ref, pn_ref, wo_ref,
                  rw_ref, rb_ref, h_ref, u_ref, rt_ref, cnt_ref, carry_sc, *, tm):
    d = D_MODEL

    @pl.when(pl.program_id(0) == 0)
    def _():
        carry_sc[...] = jnp.zeros_like(carry_sc)

    a = _dot(yg_ref[...], pg_ref[...])
    b = _dot(yn_ref[...], pn_ref[...])
    mg = _sigmoid(mg_ref[...].astype(F32))
    merged = mg[:, :d] * a + mg[:, d:] * b
    mix = _dot(merged.astype(BF16), wo_ref[...])
    h = x_ref[...] + g1_ref[0] * mix
    h_ref[...] = h
    y = h * lax.rsqrt(jnp.mean(h * h, axis=-1, keepdims=True) + RMS_EPS) * n2_ref[...]
    u = y * (1.0 + sc_ref[0]) + sh_ref[0]
    u_ref[...] = u

    lane = lax.broadcasted_iota(jnp.int32, (1, LANES), 1).astype(F32)
    logits = jnp.where(lane < N_EXPERTS, _dot3(u, rw_ref[...]) + rb_ref[...], -jnp.inf)
    vals, idxs = [], []
    cur = logits
    for _ in range(TOP_K):
        m = jnp.max(cur, axis=-1, keepdims=True)
        idx = jnp.min(jnp.where(cur == m, lane, float(LANES)), axis=-1, keepdims=True)
        vals.append(m)
        idxs.append(idx)
        cur = jnp.where(lane == idx, -jnp.inf, cur)
    es = [jnp.exp(v - vals[0]) for v in vals]
    den = es[0] + es[1] + es[2] + es[3]
    onehots = [lane == idx for idx in idxs]
    oh = jnp.zeros((tm, LANES), F32)
    for o in onehots:
        oh = oh + jnp.where(o, 1.0, 0.0)
    row = lax.broadcasted_iota(jnp.int32, (tm, tm), 0)
    col = lax.broadcasted_iota(jnp.int32, (tm, tm), 1)
    tril = jnp.where(col <= row, 1.0, 0.0).astype(BF16)
    cnt = _dot(tril, oh.astype(BF16))
    tot = cnt + carry_sc[0:1, :]
    route = jnp.zeros((tm, LANES), F32)
    for k in range(TOP_K):
        pos = jnp.sum(jnp.where(onehots[k], tot - 1.0, 0.0), axis=-1, keepdims=True)
        route = jnp.where(lane == k, es[k] / den, route)
        route = jnp.where(lane == TOP_K + k, idxs[k], route)
        route = jnp.where(lane == 2 * TOP_K + k, pos, route)
    rt_ref[...] = route
    newc = tot[tm - 1:tm, :]
    carry_sc[...] = jnp.broadcast_to(newc, carry_sc.shape)
    cnt_ref[...] = jnp.broadcast_to(newc, cnt_ref.shape)


def _merge_call(y_gla, y_nsa, main, x2, gate1, scale2, shift2, n2g, pg, pn, wo, rw, rb, T, tm=512):
    n, d = x2.shape
    tpb = T // tm
    bmap = lambda i: (i // tpb, 0, 0)
    full = lambda i: (0, 0)
    return pl.pallas_call(
        functools.partial(_merge_kernel, tm=tm),
        grid=(n // tm,),
        in_specs=[pl.BlockSpec((tm, d), lambda i: (i, 0)),
                  pl.BlockSpec((tm, d), lambda i: (i, 0)),
                  pl.BlockSpec((tm, 2 * d), lambda i: (i, C_MG // (2 * d))),
                  pl.BlockSpec((tm, d), lambda i: (i, 0)),
                  pl.BlockSpec((1, 1, d), bmap), pl.BlockSpec((1, 1, d), bmap), pl.BlockSpec((1, 1, d), bmap),
                  pl.BlockSpec((1, d), full),
                  pl.BlockSpec((d, d), full), pl.BlockSpec((d, d), full), pl.BlockSpec((d, d), full),
                  pl.BlockSpec((d, LANES), full), pl.BlockSpec((1, LANES), full)],
        out_specs=[pl.BlockSpec((tm, d), lambda i: (i, 0)),
                   pl.BlockSpec((tm, d), lambda i: (i, 0)),
                   pl.BlockSpec((tm, LANES), lambda i: (i, 0)),
                   pl.BlockSpec((8, LANES), full)],
        out_shape=[jax.ShapeDtypeStruct((n, d), F32), jax.ShapeDtypeStruct((n, d), F32),
                   jax.ShapeDtypeStruct((n, LANES), F32), jax.ShapeDtypeStruct((8, LANES), F32)],
        scratch_shapes=[pltpu.VMEM((8, LANES), F32)],
        compiler_params=_params("arbitrary"),
        name="merge",
    )(y_gla, y_nsa, main, x2, gate1, scale2, shift2, n2g, pg, pn, wo, rw, rb)


def _dispatch_kernel(dest_ref, u_ref, xz_ref, xs_ref, sem, *, td):
    del xz_ref
    base = pl.program_id(0) * (td * TOP_K)

    def row_copy(r, slot):
        return pltpu.make_async_copy(u_ref.at[pl.ds(r, 1)], xs_ref.at[pl.ds(slot, 1)], sem)

    def issue(r, c):
        for k in range(TOP_K):
            row_copy(r, dest_ref[base + r * TOP_K + k]).start()
        return c

    def drain(r, c):
        for k in range(TOP_K):
            row_copy(r, dest_ref[base + r * TOP_K + k]).wait()
        return c

    lax.fori_loop(0, td, issue, 0)
    lax.fori_loop(0, td, drain, 0)


def _dispatch_call(dest, u2, xs_zero, td=256):
    n, d = u2.shape
    return pl.pallas_call(
        functools.partial(_dispatch_kernel, td=td),
        grid_spec=pltpu.PrefetchScalarGridSpec(
            num_scalar_prefetch=1,
            grid=(n // td,),
            in_specs=[pl.BlockSpec((td, d), lambda i, dest: (i, 0)),
                      pl.BlockSpec(memory_space=pl.ANY)],
            out_specs=pl.BlockSpec(memory_space=pl.ANY),
            scratch_shapes=[pltpu.SemaphoreType.DMA(())]),
        out_shape=jax.ShapeDtypeStruct(xs_zero.shape, xs_zero.dtype),
        input_output_aliases={2: 0},
        compiler_params=_params("arbitrary"),
        name="dispatch",
    )(dest, u2, xs_zero)


def _moe_kernel(be_ref, na_ref, x_ref, w1_ref, b1_ref, w2_ref, b2_ref, y_ref, w1b, w2b):
    i = pl.program_id(0)
    e = be_ref[i]
    prev = be_ref[jnp.maximum(i - 1, 0)]
    active = i < na_ref[0]
    f = D_EXPERT

    @pl.when(active & ((i == 0) | (e != prev)))
    def _():
        w1b[...] = w1_ref[0].astype(BF16)
        w2b[...] = w2_ref[0].astype(BF16)

    @pl.when(active)
    def _():
        hb = _dot(x_ref[...].astype(BF16), w1b[...]) + b1_ref[0]
        gate = jnp.minimum(hb[:, :f], SWIGLU_LIMIT)
        up = jnp.clip(hb[:, f:], -SWIGLU_LIMIT, SWIGLU_LIMIT)
        act = gate * _sigmoid(SWIGLU_ALPHA * gate) * (up + 1.0)
        y_ref[...] = _dot(act.astype(BF16), w2b[...]) + b2_ref[0]

    @pl.when(jnp.logical_not(active))
    def _():
        y_ref[...] = jnp.zeros_like(y_ref)


def _moe_call(blk_e, nact, xs, w1, b1, w2, b2):
    p, d = xs.shape
    f2 = w1.shape[2]
    m = MOE_ROWS
    return pl.pallas_call(
        _moe_kernel,
        grid_spec=pltpu.PrefetchScalarGridSpec(
            num_scalar_prefetch=2,
            grid=(p // m,),
            in_specs=[pl.BlockSpec((m, d), lambda i, be, na: (i, 0)),
                      pl.BlockSpec((1, d, f2), lambda i, be, na: (be[i], 0, 0)),
                      pl.BlockSpec((1, 1, f2), lambda i, be, na: (be[i], 0, 0)),
                      pl.BlockSpec((1, f2 // 2, d), lambda i, be, na: (be[i], 0, 0)),
                      pl.BlockSpec((1, 1, d), lambda i, be, na: (be[i], 0, 0))],
            out_specs=pl.BlockSpec((m, d), lambda i, be, na: (i, 0)),
            scratch_shapes=[pltpu.VMEM((d, f2), BF16), pltpu.VMEM((f2 // 2, d), BF16)]),
        out_shape=jax.ShapeDtypeStruct((p, d), F32),
        compiler_params=_params("arbitrary"),
        name="moe",
    )(blk_e, nact, xs, w1, b1, w2, b2)


def _combine_kernel(dest_ref, h_ref, rt_ref, g2_ref, fg_ref, ys_ref, o_ref, buf, sems, *, tc):
    i = pl.program_id(0)
    slot = i % 2

    def row_copy(step, sl, r, k):
        src = dest_ref[step * (tc * TOP_K) + r * TOP_K + k]
        return pltpu.make_async_copy(ys_ref.at[pl.ds(src, 1)], buf.at[sl, k, pl.ds(r, 1)], sems.at[sl])

    def issue(step, sl):
        def body(r, c):
            for k in range(TOP_K):
                row_copy(step, sl, r, k).start()
            return c
        lax.fori_loop(0, tc, body, 0)

    def drain(step, sl):
        def body(r, c):
            for k in range(TOP_K):
                row_copy(step, sl, r, k).wait()
            return c
        lax.fori_loop(0, tc, body, 0)

    @pl.when(i == 0)
    def _():
        issue(0, 0)

    @pl.when(i + 1 < pl.num_programs(0))
    def _():
        issue(i + 1, 1 - slot)

    drain(i, slot)
    rt = rt_ref[...]
    moe = rt[:, 0:1] * buf[slot, 0]
    for k in range(1, TOP_K):
        moe = moe + rt[:, k:k + 1] * buf[slot, k]
    hh = h_ref[...] + g2_ref[0] * moe
    o_ref[...] = hh * lax.rsqrt(jnp.mean(hh * hh, axis=-1, keepdims=True) + RMS_EPS) * fg_ref[...]


def _combine_call(dest, h, route, gate2, fg, ys, T, tc=128):
    n, d = h.shape
    tpb = T // tc
    return pl.pallas_call(
        functools.partial(_combine_kernel, tc=tc),
        grid_spec=pltpu.PrefetchScalarGridSpec(
            num_scalar_prefetch=1,
            grid=(n // tc,),
            in_specs=[pl.BlockSpec((tc, d), lambda i, dest: (i, 0)),
                      pl.BlockSpec((tc, LANES), lambda i, dest: (i, 0)),
                      pl.BlockSpec((1, 1, d), lambda i, dest: (i // tpb, 0, 0)),
                      pl.BlockSpec((1, d), lambda i, dest: (0, 0)),
                      pl.BlockSpec(memory_space=pl.ANY)],
            out_specs=pl.BlockSpec((tc, d), lambda i, dest: (i, 0)),
            scratch_shapes=[pltpu.VMEM((2, TOP_K, tc, d), F32), pltpu.SemaphoreType.DMA((2,))]),
        out_shape=jax.ShapeDtypeStruct((n, d), F32),
        compiler_params=_params("arbitrary"),
        name="combine",
    )(dest, h, route, gate2, fg, ys)


def _mixer(x2, mod, norm1_g, w_in, gla_wa2, gla_ba, gla_norm_g, cmp_pe, cmp_w1, cmp_w2, B, T):
    d = D_MODEL
    G = NSA_KV_HEADS
    hd = NSA_HD
    shift1 = mod[:, 0 * d:1 * d].reshape(B, 1, d)
    scale1 = mod[:, 1 * d:2 * d].reshape(B, 1, d)
    w_main = jnp.concatenate([w_in[:, :_O_GA], w_in[:, _O_NQ:_O_KV], w_in[:, _O_MG:_O_END],
                              w_in[:, _O_KV:_O_NG]], axis=1).astype(BF16)
    w_small = jnp.concatenate([w_in[:, _O_GA:_O_NQ], w_in[:, _O_NG:_O_MG],
                               jnp.zeros((d, LANES - GLA_GATE_RANK - NSA_HEADS * 3), F32)], axis=1).astype(BF16)
    main, small = _inproj_call(x2, norm1_g.reshape(1, d), scale1, shift1, w_main, w_small, T)
    y_gla = _gla_call(main, small, gla_wa2, gla_ba.reshape(1, -1), gla_norm_g.reshape(1, -1), B, T)
    kv = main[:, C_KV:C_KV + 2 * G * hd].reshape(B, T // CMP_STRIDE, CMP_STRIDE, 2, G, hd)
    xkv = kv.transpose(3, 0, 4, 1, 2, 5).reshape(2, B * G, T // CMP_STRIDE, CMP_STRIDE * hd)
    kvc = _compress_call(xkv, cmp_pe.reshape(2, 1, CMP_LEN * hd), cmp_w1.astype(BF16), cmp_w2.astype(BF16))
    o_cmp, sel = _cmpsel_call(main, kvc, B, T)
    y_nsa = _nsa_call(main, small, sel, o_cmp, B, T)
    return main, y_gla, y_nsa


def _route_tables(route, counts, n):
    m = MOE_ROWS
    e = route[:, TOP_K:2 * TOP_K].astype(jnp.int32)
    pos = route[:, 2 * TOP_K:3 * TOP_K].astype(jnp.int32)
    cnt = counts[0, :N_EXPERTS].astype(jnp.int32)
    padded = ((cnt + m - 1) // m) * m
    pends = jnp.cumsum(padded)
    pstarts = pends - padded
    dest = (pstarts[e] + pos).reshape(-1)
    p_rows = n * TOP_K + N_EXPERTS * m
    nblk = p_rows // m
    starts = jnp.arange(nblk, dtype=jnp.int32) * m
    blk_e = jnp.minimum(jnp.sum((pends[None, :] <= starts[:, None]).astype(jnp.int32), axis=1), N_EXPERTS - 1)
    nact = (pends[-1:] // m).astype(jnp.int32)
    return dest, blk_e, nact, p_rows


def kernel(x, c, w_ada, b_ada, norm1_g, w_in, gla_wa2, gla_ba, gla_norm_g, cmp_pe, cmp_w1, cmp_w2, proj_gla,
           proj_nsa, w_out, norm2_g, router_w, router_b, moe_w1, moe_b1, moe_w2, moe_b2, final_g):
    B, T, d = x.shape
    n = B * T
    assert w_ada.shape[0] == 1 and d == D_MODEL and B <= 8, "single-layer block of width D_MODEL"
    l = 0
    h = x.reshape(n, d)
    c8 = jnp.concatenate([c, jnp.zeros((8 - B, d), c.dtype)], axis=0)
    mod = _mod_call(c8, w_ada[l], b_ada[l].reshape(1, -1))[:B]
    gate1 = mod[:, 2 * d:3 * d].reshape(B, 1, d)
    shift2 = mod[:, 3 * d:4 * d].reshape(B, 1, d)
    scale2 = mod[:, 4 * d:5 * d].reshape(B, 1, d)
    gate2 = mod[:, 5 * d:6 * d].reshape(B, 1, d)
    main, y_gla, y_nsa = _mixer(h, mod, norm1_g[l], w_in[l], gla_wa2[l], gla_ba[l], gla_norm_g[l],
                                cmp_pe[l], cmp_w1[l], cmp_w2[l], B, T)
    rw = jnp.concatenate([router_w[l], jnp.zeros((d, LANES - N_EXPERTS), F32)], axis=1)
    rb = jnp.concatenate([router_b[l], jnp.zeros((LANES - N_EXPERTS,), F32)]).reshape(1, LANES)
    h1, u2, route, counts = _merge_call(
        y_gla, y_nsa, main, h, gate1, scale2, shift2, norm2_g[l].reshape(1, d),
        proj_gla[l].astype(BF16), proj_nsa[l].astype(BF16), w_out[l].astype(BF16), rw, rb, T)
    dest, blk_e, nact, p_rows = _route_tables(route, counts, n)
    xs = _dispatch_call(dest, u2, jnp.zeros((p_rows, d), F32))
    ys = _moe_call(blk_e, nact, xs, moe_w1[l], moe_b1[l].reshape(N_EXPERTS, 1, -1), moe_w2[l],
                   moe_b2[l].reshape(N_EXPERTS, 1, -1))
    out = _combine_call(dest, h1, route, gate2, final_g.reshape(1, d), ys, T)
    return out.reshape(B, T, d)
```

```python
import functools

import jax
import jax.numpy as jnp
from jax import lax
from jax.experimental import pallas as pl
from jax.experimental.pallas import tpu as pltpu

F32 = jnp.float32
BF16 = jnp.bfloat16

D_MODEL = 1024
GLA_HEADS = 4
GLA_DK = 128
GLA_DV = 256
GLA_GATE_RANK = 16
GLA_TAU = 16.0
GLA_CHUNK = 64
NSA_HEADS = 8
NSA_KV_HEADS = 2
NSA_HD = 128
NSA_GROUP = NSA_HEADS // NSA_KV_HEADS
CMP_LEN = 32
CMP_STRIDE = 16
CMP_HIDDEN = 2 * NSA_HD
SEL_BLOCK = 64
SEL_TOPK = 16
WIN = 512
N_EXPERTS = 32
TOP_K = 4
D_EXPERT = D_MODEL
SWIGLU_LIMIT = 7.0
SWIGLU_ALPHA = 1.702
RMS_EPS = 1e-5

LANES = 128
NEG = -1e30
LOG2E = 1.4426950408889634
MOE_ROWS = 512
VMEM_LIMIT = 56 * 1024 * 1024

C_GQ, C_GK, C_GV, C_GR, C_NQ, C_MG, C_KV = 0, 512, 1024, 2048, 3072, 4096, 6144
MAIN_COLS = 7680
S_GA, S_NG = 0, 16

_O_GA = 3072
_O_NQ = 3088
_O_KV = 4112
_O_NG = 5648
_O_MG = 5672
_O_END = 7720


def _dot(a, b):
    return jnp.dot(a, b, preferred_element_type=F32)


def _dot_nt(a, b):
    return lax.dot_general(a, b, (((1,), (1,)), ((), ())), preferred_element_type=F32)


def _dot_tn(a, b):
    return lax.dot_general(a, b, (((0,), (0,)), ((), ())), preferred_element_type=F32)


def _split(a):
    hi = a.astype(BF16)
    lo = (a - hi.astype(F32)).astype(BF16)
    return hi, lo


def _dot3(a, b):
    ah, al = _split(a)
    bh, bl = _split(b)
    return _dot(ah, bh) + _dot(ah, bl) + _dot(al, bh)


def _sigmoid(x):
    return 1.0 / (1.0 + jnp.exp(-x))


def _params(*sem):
    return pltpu.CompilerParams(dimension_semantics=sem, vmem_limit_bytes=VMEM_LIMIT)


def _mod_kernel(c_ref, w_ref, b_ref, o_ref):
    c = c_ref[...]
    cond = c * _sigmoid(c)
    o_ref[...] = _dot3(cond, w_ref[...]) + b_ref[...]


def _mod_call(c8, w_ada, b_ada):
    d = c8.shape[1]
    n = w_ada.shape[1]
    return pl.pallas_call(
        _mod_kernel,
        grid=(n // d,),
        in_specs=[pl.BlockSpec((8, d), lambda j: (0, 0)),
                  pl.BlockSpec((d, d), lambda j: (0, j)),
                  pl.BlockSpec((1, d), lambda j: (0, j))],
        out_specs=pl.BlockSpec((8, d), lambda j: (0, j)),
        out_shape=jax.ShapeDtypeStruct((8, n), F32),
        compiler_params=_params("parallel"),
        name="mod",
    )(c8, w_ada, b_ada)


def _inproj_kernel(x_ref, g_ref, sc_ref, sh_ref, w_ref, ws_ref, o_ref, s_ref, u_sc):
    @pl.when(pl.program_id(1) == 0)
    def _():
        xf = x_ref[...]
        ms = jnp.mean(xf * xf, axis=-1, keepdims=True)
        y = xf * lax.rsqrt(ms + RMS_EPS) * g_ref[...]
        u = (y * (1.0 + sc_ref[0]) + sh_ref[0]).astype(BF16)
        u_sc[...] = u
        s_ref[...] = _dot(u, ws_ref[...])

    o_ref[...] = _dot(u_sc[...], w_ref[...]).astype(BF16)


def _inproj_call(x2, g, scale, shift, w_main, w_small, T, tm=1024, tn=1920):
    n, d = x2.shape
    tpb = T // tm
    return pl.pallas_call(
        _inproj_kernel,
        grid=(n // tm, MAIN_COLS // tn),
        in_specs=[pl.BlockSpec((tm, d), lambda i, j: (i, 0)),
                  pl.BlockSpec((1, d), lambda i, j: (0, 0)),
                  pl.BlockSpec((1, 1, d), lambda i, j: (i // tpb, 0, 0)),
                  pl.BlockSpec((1, 1, d), lambda i, j: (i // tpb, 0, 0)),
                  pl.BlockSpec((d, tn), lambda i, j: (0, j)),
                  pl.BlockSpec((d, LANES), lambda i, j: (0, 0))],
        out_specs=[pl.BlockSpec((tm, tn), lambda i, j: (i, j)),
                   pl.BlockSpec((tm, LANES), lambda i, j: (i, 0))],
        out_shape=[jax.ShapeDtypeStruct((n, MAIN_COLS), BF16),
                   jax.ShapeDtypeStruct((n, LANES), F32)],
        scratch_shapes=[pltpu.VMEM((tm, d), BF16)],
        compiler_params=_params("parallel", "arbitrary"),
        name="inproj",
    )(x2, g, scale, shift, w_main, w_small)


def _gla_kernel(q_ref, k_ref, v_ref, r_ref, s_ref, wa_ref, ba_ref, g_ref, o_ref, st_sc, *, tt):
    @pl.when(pl.program_id(1) == 0)
    def _():
        st_sc[...] = jnp.zeros_like(st_sc)

    C = GLA_CHUNK
    nc = tt // C
    dk, dv = GLA_DK, GLA_DV
    ga = s_ref[:, S_GA:S_GA + GLA_GATE_RANK]
    z = _dot3(ga, wa_ref[...]) + ba_ref[...]
    la = (jnp.minimum(z, 0.0) - jnp.log(1.0 + jnp.exp(-jnp.abs(z)))) * (1.0 / GLA_TAU)
    row = lax.broadcasted_iota(jnp.int32, (tt, tt), 0)
    col = lax.broadcasted_iota(jnp.int32, (tt, tt), 1)
    causal = ((row // C) == (col // C)) & (col <= row)
    tril = jnp.where(causal, 1.0, 0.0).astype(BF16)
    hi, lo = _split(la)
    bc_all = _dot(tril, hi) + _dot(tril, lo)
    for h in range(GLA_HEADS):
        bc = bc_all[:, h * dk:(h + 1) * dk]
        bls = [bc[(c + 1) * C - 1:(c + 1) * C, :] for c in range(nc)]
        blr = jnp.concatenate([jnp.broadcast_to(b, (C, dk)) for b in bls], axis=0)
        q = q_ref[:, h * dk:(h + 1) * dk].astype(F32) * (dk ** -0.5)
        k = k_ref[:, h * dk:(h + 1) * dk].astype(F32)
        v = v_ref[:, h * dv:(h + 1) * dv]
        qe = (q * jnp.exp(bc)).astype(BF16)
        ke = (k * jnp.exp(-bc)).astype(BF16)
        kd = (k * jnp.exp(blr - bc)).astype(BF16)
        a = jnp.where(causal, _dot_nt(qe, ke), 0.0)
        intra = _dot(a.astype(BF16), v)
        st = st_sc[h]
        inter = []
        for c in range(nc):
            sl = slice(c * C, (c + 1) * C)
            inter.append(_dot_nt(qe[sl], st.astype(BF16)))
            st = st * jnp.exp(bls[c]) + _dot_tn(v[sl], kd[sl])
        st_sc[h] = st
        o = intra + jnp.concatenate(inter, axis=0)
        y = o * lax.rsqrt(jnp.mean(o * o, axis=-1, keepdims=True) + RMS_EPS) * g_ref[...]
        r = r_ref[:, h * dv:(h + 1) * dv].astype(F32)
        o_ref[:, h * dv:(h + 1) * dv] = (y * (r * _sigmoid(r))).astype(BF16)


def _gla_call(main, small, wa2, ba, norm_g, B, T, tt=256):
    n = main.shape[0]
    nt = T // tt
    H = GLA_HEADS
    rowmap = lambda b, t: b * nt + t
    return pl.pallas_call(
        functools.partial(_gla_kernel, tt=tt),
        grid=(B, nt),
        in_specs=[pl.BlockSpec((tt, H * GLA_DK), lambda b, t: (rowmap(b, t), C_GQ // (H * GLA_DK))),
                  pl.BlockSpec((tt, H * GLA_DK), lambda b, t: (rowmap(b, t), C_GK // (H * GLA_DK))),
                  pl.BlockSpec((tt, H * GLA_DV), lambda b, t: (rowmap(b, t), C_GV // (H * GLA_DV))),
                  pl.BlockSpec((tt, H * GLA_DV), lambda b, t: (rowmap(b, t), C_GR // (H * GLA_DV))),
                  pl.BlockSpec((tt, LANES), lambda b, t: (rowmap(b, t), 0)),
                  pl.BlockSpec((GLA_GATE_RANK, H * GLA_DK), lambda b, t: (0, 0)),
                  pl.BlockSpec((1, H * GLA_DK), lambda b, t: (0, 0)),
                  pl.BlockSpec((1, GLA_DV), lambda b, t: (0, 0))],
        out_specs=pl.BlockSpec((tt, H * GLA_DV), lambda b, t: (rowmap(b, t), 0)),
        out_shape=jax.ShapeDtypeStruct((n, H * GLA_DV), BF16),
        scratch_shapes=[pltpu.VMEM((H, GLA_DV, GLA_DK), F32)],
        compiler_params=_params("parallel", "arbitrary"),
        name="gla",
    )(main, main, main, main, small, wa2, ba, norm_g)


def _compress_kernel(x_ref, pe_ref, w1_ref, w2_ref, o_ref):
    half = CMP_STRIDE * NSA_HD
    x = x_ref[0, 0]
    nb = x.shape[0]
    ylo = _dot(x, w1_ref[0, :half, :])
    yhi = _dot(x, w1_ref[0, half:, :])
    pe = jnp.broadcast_to(pe_ref[0], (8, 2 * half)).astype(BF16)
    pterm = _dot(pe, w1_ref[0])[0:1, :]
    pre = ylo + pltpu.roll(yhi, nb - 1, 0) + pterm
    hcu = pre * pre * pre
    hid = 0.5 * pre * (1.0 + jnp.tanh(0.7978845608028654 * (pre + 0.044715 * hcu)))
    o_ref[0, 0] = _dot(hid.astype(BF16), w2_ref[0]).astype(BF16)


def _compress_call(xkv, pe, w1, w2):
    _, bg, nb, wdt = xkv.shape
    return pl.pallas_call(
        _compress_kernel,
        grid=(2, bg),
        in_specs=[pl.BlockSpec((1, 1, nb, wdt), lambda s, i: (s, i, 0, 0)),
                  pl.BlockSpec((1, 1, CMP_LEN * NSA_HD), lambda s, i: (s, 0, 0)),
                  pl.BlockSpec((1, CMP_LEN * NSA_HD, CMP_HIDDEN), lambda s, i: (s, 0, 0)),
                  pl.BlockSpec((1, CMP_HIDDEN, NSA_HD), lambda s, i: (s, 0, 0))],
        out_specs=pl.BlockSpec((1, 1, nb, NSA_HD), lambda s, i: (s, i, 0, 0)),
        out_shape=jax.ShapeDtypeStruct((2, bg, nb, NSA_HD), BF16),
        compiler_params=_params("parallel", "parallel"),
        name="compress",
    )(xkv, pe, w1, w2)


def _cmpsel_kernel(q_ref, kc_ref, vc_ref, o_ref, sel_ref, *, tq, nsel):
    t0 = pl.program_id(2) * tq
    kc = kc_ref[0, 0]
    vc = vc_ref[0, 0]
    ncp = kc.shape[0]
    tpos = t0 + lax.broadcasted_iota(jnp.int32, (tq, 1), 0)
    cidx = lax.broadcasted_iota(jnp.int32, (1, ncp), 1)
    cmask = (cidx * CMP_STRIDE + (CMP_LEN - 1)) <= tpos
    psum = jnp.zeros((tq, ncp), F32)
    for r in range(NSA_GROUP):
        q = q_ref[:, r * NSA_HD:(r + 1) * NSA_HD]
        s = jnp.where(cmask, _dot_nt(q, kc) * (NSA_HD ** -0.5), NEG)
        m = jnp.max(s, axis=-1, keepdims=True)
        e = jnp.where(cmask, jnp.exp(s - m), 0.0)
        p = e / jnp.maximum(jnp.sum(e, axis=-1, keepdims=True), 1e-30)
        o_ref[:, r * NSA_HD:(r + 1) * NSA_HD] = _dot(p.astype(BF16), vc).astype(BF16)
        psum = psum + p
    ci = lax.broadcasted_iota(jnp.int32, (nsel, ncp), 1) * CMP_STRIDE
    si = lax.broadcasted_iota(jnp.int32, (nsel, ncp), 0) * SEL_BLOCK
    ov = jnp.where((ci < si + SEL_BLOCK) & (ci + CMP_LEN > si), 1.0, 0.0).astype(BF16)
    hi, lo = _split(psum)
    imp = _dot_nt(ov, hi) + _dot_nt(ov, lo)
    tlane = t0 + lax.broadcasted_iota(jnp.int32, (1, tq), 1)
    jblk = lax.broadcasted_iota(jnp.int32, (nsel, 1), 0)
    tblk = tlane // SEL_BLOCK
    forced = (jblk == 0) | (jblk == tblk) | (jblk == tblk - 1)
    valid = (jblk * SEL_BLOCK) <= tlane
    score = jnp.where(forced, jnp.inf, jnp.where(valid, imp, -jnp.inf))
    rank = jnp.zeros((nsel, tq), F32)
    for i in range(nsel):
        si_ = score[i:i + 1, :]
        tie = jnp.where(jblk > i, 1.0, 0.0)
        rank = rank + jnp.where(si_ > score, 1.0, jnp.where(si_ == score, tie, 0.0))
    ntop = min(SEL_TOPK, nsel)
    chosen = jnp.where(valid & (rank < ntop), 1.0, 0.0)
    sel_ref[0, 0] = jnp.transpose(chosen).astype(BF16)


def _cmpsel_call(main, kvc, B, T, tq=256):
    n = main.shape[0]
    nq = T // tq
    nsel = T // SEL_BLOCK
    ncp = kvc.shape[2]
    G = NSA_KV_HEADS
    qw = NSA_GROUP * NSA_HD
    return pl.pallas_call(
        functools.partial(_cmpsel_kernel, tq=tq, nsel=nsel),
        grid=(B, G, nq),
        in_specs=[pl.BlockSpec((tq, qw), lambda b, g, t: (b * nq + t, C_NQ // qw + g)),
                  pl.BlockSpec((1, 1, ncp, NSA_HD), lambda b, g, t: (0, b * G + g, 0, 0)),
                  pl.BlockSpec((1, 1, ncp, NSA_HD), lambda b, g, t: (1, b * G + g, 0, 0))],
        out_specs=[pl.BlockSpec((tq, qw), lambda b, g, t: (b * nq + t, g)),
                   pl.BlockSpec((1, 1, tq, nsel), lambda b, g, t: (b, g, t, 0))],
        out_shape=[jax.ShapeDtypeStruct((n, NSA_HEADS * NSA_HD), BF16),
                   jax.ShapeDtypeStruct((B, G, T, nsel), BF16)],
        compiler_params=_params("parallel", "parallel", "parallel"),
        name="cmpsel",
    )(main, kvc, kvc)


def _nsa_kernel(q_ref, ks_ref, vs_ref, kw_ref, vw_ref, sel_ref, oc_ref, s_ref, o_ref, s_sc, *, tq, tk, T, nsel):
    R = NSA_GROUP
    hd = NSA_HD
    g = pl.program_id(1)
    t0 = pl.program_id(2) * tq
    qscale = (hd ** -0.5) * LOG2E
    q4 = jnp.concatenate([q_ref[:, r * hd:(r + 1) * hd] for r in range(R)], axis=0)
    q4 = (q4.astype(F32) * qscale).astype(BF16)
    rowpos = t0 + lax.broadcasted_iota(jnp.int32, (tq, 1), 0)
    selm = sel_ref[0, 0].astype(F32) - 1.0
    selm4 = jnp.concatenate([selm] * R, axis=0).astype(BF16)

    qa = jnp.concatenate([q4, selm4], axis=1)
    jcol = lax.broadcasted_iota(jnp.int32, (tk, nsel), 1)
    crow = lax.broadcasted_iota(jnp.int32, (tk, nsel), 0) // SEL_BLOCK
    kcol = lax.broadcasted_iota(jnp.int32, (1, tk), 1)

    def scores(kt):
        k0 = pl.multiple_of(kt * tk, tk)
        ind = jnp.where(jcol == crow + kt * (tk // SEL_BLOCK), 1e30, 0.0).astype(BF16)
        s_sc[kt % 2] = _dot_nt(qa, jnp.concatenate([ks_ref[pl.ds(k0, tk), :], ind], axis=1))

    def attend(kt, carry, diagonal):
        m, l, acc = carry
        k0 = pl.multiple_of(kt * tk, tk)
        s = s_sc[kt % 2].reshape(R, tq, tk)
        if diagonal:
            s = s + jnp.where((k0 + kcol) <= rowpos, 0.0, NEG)[None]
        m_new = jnp.maximum(m, jnp.max(s, axis=-1, keepdims=True))
        alpha = jnp.exp2(m - m_new)
        p = jnp.exp2(s - m_new)
        l = alpha * l + jnp.sum(p, axis=-1, keepdims=True)
        pv = _dot(p.reshape(R * tq, tk).astype(BF16), vs_ref[pl.ds(k0, tk), :]).reshape(R, tq, hd)
        return m_new, l, alpha * acc + pv

    def step(kt, carry):
        carry = attend(kt, carry, False)
        scores(kt + 1)
        return carry

    init = (jnp.full((R, tq, 1), NEG, F32), jnp.zeros((R, tq, 1), F32), jnp.zeros((R, tq, hd), F32))
    kdiag = t0 // tk
    scores(0)
    carry = lax.fori_loop(0, kdiag, step, init)
    _, l_s, acc_s = attend(kdiag, carry, True)
    o_sel = acc_s / l_s

    wlen = WIN + tq
    w0 = pl.multiple_of(jnp.maximum(t0 - WIN, 0), tq)
    kw = kw_ref[pl.ds(w0, wlen), :]
    vw = vw_ref[pl.ds(w0, wlen), :]
    kpos = w0 + lax.broadcasted_iota(jnp.int32, (1, wlen), 1)
    wbias = jnp.where((kpos <= rowpos) & (rowpos - kpos < WIN), 0.0, NEG)
    sw = _dot_nt(q4, kw).reshape(R, tq, wlen) + wbias[None]
    mw = jnp.max(sw, axis=-1, keepdims=True)
    pw = jnp.exp2(sw - mw)
    lw = jnp.sum(pw, axis=-1, keepdims=True)
    o_win = _dot(pw.reshape(R * tq, wlen).astype(BF16), vw).reshape(R, tq, hd) / lw

    ng = R * 3
    graw = jnp.where(g == 0, s_ref[:, S_NG:S_NG + ng], s_ref[:, S_NG + ng:S_NG + 2 * ng])
    gates = _sigmoid(graw)
    for r in range(R):
        oc = oc_ref[:, r * hd:(r + 1) * hd].astype(F32)
        out = (gates[:, 3 * r:3 * r + 1] * oc + gates[:, 3 * r + 1:3 * r + 2] * o_sel[r]
               + gates[:, 3 * r + 2:3 * r + 3] * o_win[r])
        o_ref[:, r * hd:(r + 1) * hd] = out.astype(BF16)


def _nsa_call(main, small, sel, o_cmp, B, T, tq=128, tk=512):
    n = main.shape[0]
    nq = T // tq
    nsel = T // SEL_BLOCK
    G = NSA_KV_HEADS
    hd = NSA_HD
    qw = NSA_GROUP * hd
    tk = min(tk, T)
    kvb = C_KV // hd

    def kvspec(slot):
        return pl.BlockSpec((T, hd), lambda b, g, t: (b, kvb + 2 * slot + g))

    return pl.pallas_call(
        functools.partial(_nsa_kernel, tq=tq, tk=tk, T=T, nsel=nsel),
        grid=(B, G, nq),
        in_specs=[pl.BlockSpec((tq, qw), lambda b, g, t: (b * nq + t, C_NQ // qw + g)),
                  kvspec(2), kvspec(3), kvspec(4), kvspec(5),
                  pl.BlockSpec((1, 1, tq, nsel), lambda b, g, t: (b, g, t, 0)),
                  pl.BlockSpec((tq, qw), lambda b, g, t: (b * nq + t, g)),
                  pl.BlockSpec((tq, LANES), lambda b, g, t: (b * nq + t, 0))],
        out_specs=pl.BlockSpec((tq, qw), lambda b, g, t: (b * nq + t, g)),
        out_shape=jax.ShapeDtypeStruct((n, NSA_HEADS * hd), BF16),
        scratch_shapes=[pltpu.VMEM((2, NSA_GROUP * tq, tk), F32)],
        compiler_params=_params("parallel", "parallel", "parallel"),
        name="nsa",
    )(main, main, main, main, main, sel, o_cmp, small)


def _merge_kernel(yg_ref, yn_ref, mg_ref, x_ref, g1_ref, sc_ref, sh_ref, n2_ref, pg_ref, pn_ref, wo_ref,
                  rw_ref, rb_ref, h_ref, u_ref, rt_ref, cnt_ref, carry_sc, *, tm):
    d = D_MODEL

    @pl.when(pl.program_id(0) == 0)
    def _():
        carry_sc[...] = jnp.zeros_like(carry_sc)

    a = _dot(yg_ref[...], pg_ref[...])
    b = _dot(yn_ref[...], pn_ref[...])
    mg = _sigmoid(mg_ref[...].astype(F32))
    merged = mg[:, :d] * a + mg[:, d:] * b
    mix = _dot(merged.astype(BF16), wo_ref[...])
    h = x_ref[...] + g1_ref[0] * mix
    h_ref[...] = h
    y = h * lax.rsqrt(jnp.mean(h * h, axis=-1, keepdims=True) + RMS_EPS) * n2_ref[...]
    u = y * (1.0 + sc_ref[0]) + sh_ref[0]
    u_ref[...] = u

    lane = lax.broadcasted_iota(jnp.int32, (1, LANES), 1).astype(F32)
    logits = jnp.where(lane < N_EXPERTS, _dot3(u, rw_ref[...]) + rb_ref[...], -jnp.inf)
    vals, idxs = [], []
    cur = logits
    for _ in range(TOP_K):
        m = jnp.max(cur, axis=-1, keepdims=True)
        idx = jnp.min(jnp.where(cur == m, lane, float(LANES)), axis=-1, keepdims=True)
        vals.append(m)
        idxs.append(idx)
        cur = jnp.where(lane == idx, -jnp.inf, cur)
    es = [jnp.exp(v - vals[0]) for v in vals]
    den = es[0] + es[1] + es[2] + es[3]
    onehots = [lane == idx for idx in idxs]
    oh = jnp.zeros((tm, LANES), F32)
    for o in onehots:
        oh = oh + jnp.where(o, 1.0, 0.0)
    row = lax.broadcasted_iota(jnp.int32, (tm, tm), 0)
    col = lax.broadcasted_iota(jnp.int32, (tm, tm), 1)
    tril = jnp.where(col <= row, 1.0, 0.0).astype(BF16)
    cnt = _dot(tril, oh.astype(BF16))
    tot = cnt + carry_sc[0:1, :]
    route = jnp.zeros((tm, LANES), F32)
    for k in range(TOP_K):
        pos = jnp.sum(jnp.where(onehots[k], tot - 1.0, 0.0), axis=-1, keepdims=True)
        route = jnp.where(lane == k, es[k] / den, route)
        route = jnp.where(lane == TOP_K + k, idxs[k], route)
        route = jnp.where(lane == 2 * TOP_K + k, pos, route)
    rt_ref[...] = route
    newc = tot[tm - 1:tm, :]
    carry_sc[...] = jnp.broadcast_to(newc, carry_sc.shape)
    cnt_ref[...] = jnp.broadcast_to(newc, cnt_ref.shape)


def _merge_call(y_gla, y_nsa, main, x2, gate1, scale2, shift2, n2g, pg, pn, wo, rw, rb, T, tm=512):
    n, d = x2.shape
    tpb = T // tm
    bmap = lambda i: (i // tpb, 0, 0)
    full = lambda i: (0, 0)
    return pl.pallas_call(
        functools.partial(_merge_kernel, tm=tm),
        grid=(n // tm,),
        in_specs=[pl.BlockSpec((tm, d), lambda i: (i, 0)),
                  pl.BlockSpec((tm, d), lambda i: (i, 0)),
                  pl.BlockSpec((tm, 2 * d), lambda i: (i, C_MG // (2 * d))),
                  pl.BlockSpec((tm, d), lambda i: (i, 0)),
                  pl.BlockSpec((1, 1, d), bmap), pl.BlockSpec((1, 1, d), bmap), pl.BlockSpec((1, 1, d), bmap),
                  pl.BlockSpec((1, d), full),
                  pl.BlockSpec((d, d), full), pl.BlockSpec((d, d), full), pl.BlockSpec((d, d), full),
                  pl.BlockSpec((d, LANES), full), pl.BlockSpec((1, LANES), full)],
        out_specs=[pl.BlockSpec((tm, d), lambda i: (i, 0)),
                   pl.BlockSpec((tm, d), lambda i: (i, 0)),
                   pl.BlockSpec((tm, LANES), lambda i: (i, 0)),
                   pl.BlockSpec((8, LANES), full)],
        out_shape=[jax.ShapeDtypeStruct((n, d), F32), jax.ShapeDtypeStruct((n, d), F32),
                   jax.ShapeDtypeStruct((n, LANES), F32), jax.ShapeDtypeStruct((8, LANES), F32)],
        scratch_shapes=[pltpu.VMEM((8, LANES), F32)],
        compiler_params=_params("arbitrary"),
        name="merge",
    )(y_gla, y_nsa, main, x2, gate1, scale2, shift2, n2g, pg, pn, wo, rw, rb)


def _dispatch_kernel(dest_ref, pend_ref, u_ref, xs_ref, zbuf, sem, zsem, *, td):
    base = pl.program_id(0) * (td * TOP_K)

    @pl.when(pl.program_id(0) == 0)
    def _():
        zbuf[...] = jnp.zeros_like(zbuf)

        def last_block(e):
            end = pend_ref[e]
            prev = pend_ref[e - 1] if e else 0
            start = pl.multiple_of(jnp.maximum(end - MOE_ROWS, 0), MOE_ROWS)
            return end > prev, pltpu.make_async_copy(zbuf, xs_ref.at[pl.ds(start, MOE_ROWS)], zsem)

        def unused_block(b):
            return b * MOE_ROWS >= pend_ref[N_EXPERTS - 1], pltpu.make_async_copy(
                zbuf, xs_ref.at[pl.ds(b * MOE_ROWS, MOE_ROWS)], zsem)

        nblk = xs_ref.shape[0] // MOE_ROWS
        fills = [functools.partial(last_block, e) for e in range(N_EXPERTS)]
        fills += [functools.partial(unused_block, b) for b in range(nblk - N_EXPERTS, nblk)]
        for f in fills:
            cond, fill = f()
            pl.when(cond)(fill.start)
        for f in fills:
            cond, fill = f()
            pl.when(cond)(fill.wait)

    def row_copy(r, slot):
        return pltpu.make_async_copy(u_ref.at[pl.ds(r, 1)], xs_ref.at[pl.ds(slot, 1)], sem)

    def issue(r, c):
        for k in range(TOP_K):
            row_copy(r, dest_ref[base + r * TOP_K + k]).start(priority=k % 2)
        return c

    def drain(r, c):
        for k in range(TOP_K):
            row_copy(r, dest_ref[base + r * TOP_K + k]).wait()
        return c

    lax.fori_loop(0, td, issue, 0)
    lax.fori_loop(0, td, drain, 0)


def _dispatch_call(dest, pends, u2, p_rows, td=256):
    n, d = u2.shape
    return pl.pallas_call(
        functools.partial(_dispatch_kernel, td=td),
        grid_spec=pltpu.PrefetchScalarGridSpec(
            num_scalar_prefetch=2,
            grid=(n // td,),
            in_specs=[pl.BlockSpec((td, d), lambda i, dest, pends: (i, 0))],
            out_specs=pl.BlockSpec(memory_space=pl.ANY),
            scratch_shapes=[pltpu.VMEM((MOE_ROWS, d), F32), pltpu.SemaphoreType.DMA(()),
                            pltpu.SemaphoreType.DMA(())]),
        out_shape=jax.ShapeDtypeStruct((p_rows, d), F32),
        compiler_params=_params("arbitrary"),
        name="dispatch",
    )(dest, pends, u2)


def _moe_kernel(be_ref, na_ref, x_ref, w1_ref, b1_ref, w2_ref, b2_ref, y_ref, w1b, w2b):
    i = pl.program_id(0)
    e = be_ref[i]
    prev = be_ref[jnp.maximum(i - 1, 0)]
    active = i < na_ref[0]
    f = D_EXPERT

    @pl.when(active & ((i == 0) | (e != prev)))
    def _():
        w1b[...] = w1_ref[0].astype(BF16)
        w2b[...] = w2_ref[0].astype(BF16)

    @pl.when(active)
    def _():
        hb = _dot(x_ref[...].astype(BF16), w1b[...]) + b1_ref[0]
        gate = jnp.minimum(hb[:, :f], SWIGLU_LIMIT)
        up = jnp.clip(hb[:, f:], -SWIGLU_LIMIT, SWIGLU_LIMIT)
        act = gate * _sigmoid(SWIGLU_ALPHA * gate) * (up + 1.0)
        y_ref[...] = _dot(act.astype(BF16), w2b[...]) + b2_ref[0]

    @pl.when(jnp.logical_not(active))
    def _():
        y_ref[...] = jnp.zeros_like(y_ref)


def _moe_call(blk_e, nact, xs, w1, b1, w2, b2):
    p, d = xs.shape
    f2 = w1.shape[2]
    m = MOE_ROWS
    return pl.pallas_call(
        _moe_kernel,
        grid_spec=pltpu.PrefetchScalarGridSpec(
            num_scalar_prefetch=2,
            grid=(p // m,),
            in_specs=[pl.BlockSpec((m, d), lambda i, be, na: (i, 0)),
                      pl.BlockSpec((1, d, f2), lambda i, be, na: (be[i], 0, 0)),
                      pl.BlockSpec((1, 1, f2), lambda i, be, na: (be[i], 0, 0)),
                      pl.BlockSpec((1, f2 // 2, d), lambda i, be, na: (be[i], 0, 0)),
                      pl.BlockSpec((1, 1, d), lambda i, be, na: (be[i], 0, 0))],
            out_specs=pl.BlockSpec((m, d), lambda i, be, na: (i, 0)),
            scratch_shapes=[pltpu.VMEM((d, f2), BF16), pltpu.VMEM((f2 // 2, d), BF16)]),
        out_shape=jax.ShapeDtypeStruct((p, d), F32),
        compiler_params=_params("arbitrary"),
        name="moe",
    )(blk_e, nact, xs, w1, b1, w2, b2)


def _combine_kernel(dest_ref, h_ref, rt_ref, g2_ref, fg_ref, ys_ref, o_ref, buf, sems, *, tc):
    i = pl.program_id(0)
    slot = i % 2

    def row_copy(step, sl, r, k):
        src = dest_ref[step * (tc * TOP_K) + r * TOP_K + k]
        return pltpu.make_async_copy(ys_ref.at[pl.ds(src, 1)], buf.at[sl, k, pl.ds(r, 1)], sems.at[sl])

    def issue(step, sl):
        def body(r, c):
            for k in range(TOP_K):
                row_copy(step, sl, r, k).start(priority=k % 2)
            return c
        lax.fori_loop(0, tc, body, 0)

    def drain(step, sl):
        def body(r, c):
            for k in range(TOP_K):
                row_copy(step, sl, r, k).wait()
            return c
        lax.fori_loop(0, tc, body, 0)

    @pl.when(i == 0)
    def _():
        issue(0, 0)

    @pl.when(i + 1 < pl.num_programs(0))
    def _():
        issue(i + 1, 1 - slot)

    drain(i, slot)
    rt = rt_ref[...]
    moe = rt[:, 0:1] * buf[slot, 0]
    for k in range(1, TOP_K):
        moe = moe + rt[:, k:k + 1] * buf[slot, k]
    hh = h_ref[...] + g2_ref[0] * moe
    o_ref[...] = hh * lax.rsqrt(jnp.mean(hh * hh, axis=-1, keepdims=True) + RMS_EPS) * fg_ref[...]


def _combine_call(dest, h, route, gate2, fg, ys, T, tc=128):
    n, d = h.shape
    tpb = T // tc
    return pl.pallas_call(
        functools.partial(_combine_kernel, tc=tc),
        grid_spec=pltpu.PrefetchScalarGridSpec(
            num_scalar_prefetch=1,
            grid=(n // tc,),
            in_specs=[pl.BlockSpec((tc, d), lambda i, dest: (i, 0)),
                      pl.BlockSpec((tc, LANES), lambda i, dest: (i, 0)),
                      pl.BlockSpec((1, 1, d), lambda i, dest: (i // tpb, 0, 0)),
                      pl.BlockSpec((1, d), lambda i, dest: (0, 0)),
                      pl.BlockSpec(memory_space=pl.ANY)],
            out_specs=pl.BlockSpec((tc, d), lambda i, dest: (i, 0)),
            scratch_shapes=[pltpu.VMEM((2, TOP_K, tc, d), F32), pltpu.SemaphoreType.DMA((2,))]),
        out_shape=jax.ShapeDtypeStruct((n, d), F32),
        compiler_params=_params("arbitrary"),
        name="combine",
    )(dest, h, route, gate2, fg, ys)


def _mixer(x2, mod, norm1_g, w_in, gla_wa2, gla_ba, gla_norm_g, cmp_pe, cmp_w1, cmp_w2, B, T):
    d = D_MODEL
    G = NSA_KV_HEADS
    hd = NSA_HD
    shift1 = mod[:, 0 * d:1 * d].reshape(B, 1, d)
    scale1 = mod[:, 1 * d:2 * d].reshape(B, 1, d)
    w_main = jnp.concatenate([w_in[:, :_O_GA], w_in[:, _O_NQ:_O_KV], w_in[:, _O_MG:_O_END],
                              w_in[:, _O_KV:_O_NG]], axis=1).astype(BF16)
    w_small = jnp.concatenate([w_in[:, _O_GA:_O_NQ], w_in[:, _O_NG:_O_MG],
                               jnp.zeros((d, LANES - GLA_GATE_RANK - NSA_HEADS * 3), F32)], axis=1).astype(BF16)
    main, small = _inproj_call(x2, norm1_g.reshape(1, d), scale1, shift1, w_main, w_small, T)
    y_gla = _gla_call(main, small, gla_wa2, gla_ba.reshape(1, -1), gla_norm_g.reshape(1, -1), B, T)
    kv = main[:, C_KV:C_KV + 2 * G * hd].reshape(B, T // CMP_STRIDE, CMP_STRIDE, 2, G, hd)
    xkv = kv.transpose(3, 0, 4, 1, 2, 5).reshape(2, B * G, T // CMP_STRIDE, CMP_STRIDE * hd)
    kvc = _compress_call(xkv, cmp_pe.reshape(2, 1, CMP_LEN * hd), cmp_w1.astype(BF16), cmp_w2.astype(BF16))
    o_cmp, sel = _cmpsel_call(main, kvc, B, T)
    y_nsa = _nsa_call(main, small, sel, o_cmp, B, T)
    return main, y_gla, y_nsa


def _route_tables(route, counts, n):
    m = MOE_ROWS
    e = route[:, TOP_K:2 * TOP_K].astype(jnp.int32)
    pos = route[:, 2 * TOP_K:3 * TOP_K].astype(jnp.int32)
    cnt = counts[0, :N_EXPERTS].astype(jnp.int32)
    padded = ((cnt + m - 1) // m) * m
    pends = jnp.cumsum(padded)
    pstarts = pends - padded
    dest = (pstarts[e] + pos).reshape(-1)
    p_rows = n * TOP_K + N_EXPERTS * m
    nblk = p_rows // m
    starts = jnp.arange(nblk, dtype=jnp.int32) * m
    blk_e = jnp.minimum(jnp.sum((pends[None, :] <= starts[:, None]).astype(jnp.int32), axis=1), N_EXPERTS - 1)
    nact = (pends[-1:] // m).astype(jnp.int32)
    return dest, pends.astype(jnp.int32), blk_e, nact, p_rows


def kernel(x, c, w_ada, b_ada, norm1_g, w_in, gla_wa2, gla_ba, gla_norm_g, cmp_pe, cmp_w1, cmp_w2, proj_gla,
           proj_nsa, w_out, norm2_g, router_w, router_b, moe_w1, moe_b1, moe_w2, moe_b2, final_g):
    B, T, d = x.shape
    n = B * T
    assert w_ada.shape[0] == 1 and d == D_MODEL and B <= 8, "single-layer block of width D_MODEL"
    l = 0
    h = x.reshape(n, d)
    c8 = jnp.concatenate([c, jnp.zeros((8 - B, d), c.dtype)], axis=0)
    mod = _mod_call(c8, w_ada[l], b_ada[l].reshape(1, -1))[:B]
    gate1 = mod[:, 2 * d:3 * d].reshape(B, 1, d)
    shift2 = mod[:, 3 * d:4 * d].reshape(B, 1, d)
    scale2 = mod[:, 4 * d:5 * d].reshape(B, 1, d)
    gate2 = mod[:, 5 * d:6 * d].reshape(B, 1, d)
    main, y_gla, y_nsa = _mixer(h, mod, norm1_g[l], w_in[l], gla_wa2[l], gla_ba[l], gla_norm_g[l],
                                cmp_pe[l], cmp_w1[l], cmp_w2[l], B, T)
    rw = jnp.concatenate([router_w[l], jnp.zeros((d, LANES - N_EXPERTS), F32)], axis=1)
    rb = jnp.concatenate([router_b[l], jnp.zeros((LANES - N_EXPERTS,), F32)]).reshape(1, LANES)
    h1, u2, route, counts = _merge_call(
        y_gla, y_nsa, main, h, gate1, scale2, shift2, norm2_g[l].reshape(1, d),
        proj_gla[l].astype(BF16), proj_nsa[l].astype(BF16), w_out[l].astype(BF16), rw, rb, T)
    dest, pends, blk_e, nact, p_rows = _route_tables(route, counts, n)
    xs = _dispatch_call(dest, pends, u2, p_rows)
    ys = _moe_call(blk_e, nact, xs, moe_w1[l], moe_b1[l].reshape(N_EXPERTS, 1, -1), moe_w2[l],
                   moe_b2[l].reshape(N_EXPERTS, 1, -1))
    out = _combine_call(dest, h1, route, gate2, final_g.reshape(1, d), ys, T)
    return out.reshape(B, T, d)
```

```python
import functools

import jax
import jax.numpy as jnp
from jax import lax
from jax.experimental import pallas as pl
from jax.experimental.pallas import tpu as pltpu

F32 = jnp.float32
BF16 = jnp.bfloat16

D_MODEL = 1024
GLA_HEADS = 4
GLA_DK = 128
GLA_DV = 256
GLA_GATE_RANK = 16
GLA_TAU = 16.0
GLA_CHUNK = 64
NSA_HEADS = 8
NSA_KV_HEADS = 2
NSA_HD = 128
NSA_GROUP = NSA_HEADS // NSA_KV_HEADS
CMP_LEN = 32
CMP_STRIDE = 16
CMP_HIDDEN = 2 * NSA_HD
SEL_BLOCK = 64
SEL_TOPK = 16
WIN = 512
N_EXPERTS = 32
TOP_K = 4
D_EXPERT = D_MODEL
SWIGLU_LIMIT = 7.0
SWIGLU_ALPHA = 1.702
RMS_EPS = 1e-5

LANES = 128
NEG = -1e30
LOG2E = 1.4426950408889634
MOE_ROWS = 512
VMEM_LIMIT = 56 * 1024 * 1024

C_GQ, C_GK, C_GV, C_GR, C_NQ, C_MG, C_KV = 0, 512, 1024, 2048, 3072, 4096, 6144
MAIN_COLS = 7680
S_GA, S_NG = 0, 16

_O_GA = 3072
_O_NQ = 3088
_O_KV = 4112
_O_NG = 5648
_O_MG = 5672
_O_END = 7720


def _dot(a, b):
    return jnp.dot(a, b, preferred_element_type=F32)


def _dot_nt(a, b):
    return lax.dot_general(a, b, (((1,), (1,)), ((), ())), preferred_element_type=F32)


def _dot_tn(a, b):
    return lax.dot_general(a, b, (((0,), (0,)), ((), ())), preferred_element_type=F32)


def _split(a):
    hi = a.astype(BF16)
    lo = (a - hi.astype(F32)).astype(BF16)
    return hi, lo


def _dot3(a, b):
    ah, al = _split(a)
    bh, bl = _split(b)
    return _dot(ah, bh) + _dot(ah, bl) + _dot(al, bh)


def _sigmoid(x):
    return 1.0 / (1.0 + jnp.exp(-x))


def _params(*sem):
    return pltpu.CompilerParams(dimension_semantics=sem, vmem_limit_bytes=VMEM_LIMIT)


def _mod_kernel(c_ref, w_ref, b_ref, o_ref):
    c = c_ref[...]
    cond = c * _sigmoid(c)
    o_ref[...] = _dot3(cond, w_ref[...]) + b_ref[...]


def _mod_call(c8, w_ada, b_ada):
    d = c8.shape[1]
    n = w_ada.shape[1]
    return pl.pallas_call(
        _mod_kernel,
        grid=(n // d,),
        in_specs=[pl.BlockSpec((8, d), lambda j: (0, 0)),
                  pl.BlockSpec((d, d), lambda j: (0, j)),
                  pl.BlockSpec((1, d), lambda j: (0, j))],
        out_specs=pl.BlockSpec((8, d), lambda j: (0, j)),
        out_shape=jax.ShapeDtypeStruct((8, n), F32),
        compiler_params=_params("parallel"),
        name="mod",
    )(c8, w_ada, b_ada)


def _inproj_kernel(x_ref, g_ref, sc_ref, sh_ref, w_ref, ws_ref, o_ref, s_ref, u_sc):
    @pl.when(pl.program_id(1) == 0)
    def _():
        xf = x_ref[...]
        ms = jnp.mean(xf * xf, axis=-1, keepdims=True)
        y = xf * lax.rsqrt(ms + RMS_EPS) * g_ref[...]
        u = (y * (1.0 + sc_ref[0]) + sh_ref[0]).astype(BF16)
        u_sc[...] = u
        s_ref[...] = _dot(u, ws_ref[...])

    o_ref[...] = _dot(u_sc[...], w_ref[...]).astype(BF16)


def _inproj_call(x2, g, scale, shift, w_main, w_small, T, tm=1024, tn=1920):
    n, d = x2.shape
    tpb = T // tm
    return pl.pallas_call(
        _inproj_kernel,
        grid=(n // tm, MAIN_COLS // tn),
        in_specs=[pl.BlockSpec((tm, d), lambda i, j: (i, 0)),
                  pl.BlockSpec((1, d), lambda i, j: (0, 0)),
                  pl.BlockSpec((1, 1, d), lambda i, j: (i // tpb, 0, 0)),
                  pl.BlockSpec((1, 1, d), lambda i, j: (i // tpb, 0, 0)),
                  pl.BlockSpec((d, tn), lambda i, j: (0, j)),
                  pl.BlockSpec((d, LANES), lambda i, j: (0, 0))],
        out_specs=[pl.BlockSpec((tm, tn), lambda i, j: (i, j)),
                   pl.BlockSpec((tm, LANES), lambda i, j: (i, 0))],
        out_shape=[jax.ShapeDtypeStruct((n, MAIN_COLS), BF16),
                   jax.ShapeDtypeStruct((n, LANES), F32)],
        scratch_shapes=[pltpu.VMEM((tm, d), BF16)],
        compiler_params=_params("parallel", "arbitrary"),
        name="inproj",
    )(x2, g, scale, shift, w_main, w_small)


def _gla_kernel(q_ref, k_ref, v_ref, r_ref, s_ref, wa_ref, ba_ref, g_ref, o_ref, st_sc, *, tt):
    @pl.when(pl.program_id(1) == 0)
    def _():
        st_sc[...] = jnp.zeros_like(st_sc)

    C = GLA_CHUNK
    nc = tt // C
    dk, dv = GLA_DK, GLA_DV
    ga = s_ref[:, S_GA:S_GA + GLA_GATE_RANK]
    z = _dot3(ga, wa_ref[...]) + ba_ref[...]
    la = (jnp.minimum(z, 0.0) - jnp.log(1.0 + jnp.exp(-jnp.abs(z)))) * (1.0 / GLA_TAU)
    row = lax.broadcasted_iota(jnp.int32, (tt, tt), 0)
    col = lax.broadcasted_iota(jnp.int32, (tt, tt), 1)
    causal = ((row // C) == (col // C)) & (col <= row)
    tril = jnp.where(causal, 1.0, 0.0).astype(BF16)
    hi, lo = _split(la)
    bc_all = _dot(tril, hi) + _dot(tril, lo)
    for h in range(GLA_HEADS):
        bc = bc_all[:, h * dk:(h + 1) * dk]
        bls = [bc[(c + 1) * C - 1:(c + 1) * C, :] for c in range(nc)]
        blr = jnp.concatenate([jnp.broadcast_to(b, (C, dk)) for b in bls], axis=0)
        q = q_ref[:, h * dk:(h + 1) * dk].astype(F32) * (dk ** -0.5)
        k = k_ref[:, h * dk:(h + 1) * dk].astype(F32)
        v = v_ref[:, h * dv:(h + 1) * dv]
        qe = (q * jnp.exp(bc)).astype(BF16)
        ke = (k * jnp.exp(-bc)).astype(BF16)
        kd = (k * jnp.exp(blr - bc)).astype(BF16)
        a = jnp.where(causal, _dot_nt(qe, ke), 0.0)
        intra = _dot(a.astype(BF16), v)
        st = st_sc[h]
        inter = []
        for c in range(nc):
            sl = slice(c * C, (c + 1) * C)
            inter.append(_dot_nt(qe[sl], st.astype(BF16)))
            st = st * jnp.exp(bls[c]) + _dot_tn(v[sl], kd[sl])
        st_sc[h] = st
        o = intra + jnp.concatenate(inter, axis=0)
        y = o * lax.rsqrt(jnp.mean(o * o, axis=-1, keepdims=True) + RMS_EPS) * g_ref[...]
        r = r_ref[:, h * dv:(h + 1) * dv].astype(F32)
        o_ref[:, h * dv:(h + 1) * dv] = (y * (r * _sigmoid(r))).astype(BF16)


def _gla_call(main, small, wa2, ba, norm_g, B, T, tt=256):
    n = main.shape[0]
    nt = T // tt
    H = GLA_HEADS
    rowmap = lambda b, t: b * nt + t
    return pl.pallas_call(
        functools.partial(_gla_kernel, tt=tt),
        grid=(B, nt),
        in_specs=[pl.BlockSpec((tt, H * GLA_DK), lambda b, t: (rowmap(b, t), C_GQ // (H * GLA_DK))),
                  pl.BlockSpec((tt, H * GLA_DK), lambda b, t: (rowmap(b, t), C_GK // (H * GLA_DK))),
                  pl.BlockSpec((tt, H * GLA_DV), lambda b, t: (rowmap(b, t), C_GV // (H * GLA_DV))),
                  pl.BlockSpec((tt, H * GLA_DV), lambda b, t: (rowmap(b, t), C_GR // (H * GLA_DV))),
                  pl.BlockSpec((tt, LANES), lambda b, t: (rowmap(b, t), 0)),
                  pl.BlockSpec((GLA_GATE_RANK, H * GLA_DK), lambda b, t: (0, 0)),
                  pl.BlockSpec((1, H * GLA_DK), lambda b, t: (0, 0)),
                  pl.BlockSpec((1, GLA_DV), lambda b, t: (0, 0))],
        out_specs=pl.BlockSpec((tt, H * GLA_DV), lambda b, t: (rowmap(b, t), 0)),
        out_shape=jax.ShapeDtypeStruct((n, H * GLA_DV), BF16),
        scratch_shapes=[pltpu.VMEM((H, GLA_DV, GLA_DK), F32)],
        compiler_params=_params("parallel", "arbitrary"),
        name="gla",
    )(main, main, main, main, small, wa2, ba, norm_g)


def _compress_kernel(x_ref, pe_ref, w1_ref, w2_ref, o_ref):
    half = CMP_STRIDE * NSA_HD
    x = x_ref[0, 0]
    nb = x.shape[0]
    ylo = _dot(x, w1_ref[0, :half, :])
    yhi = _dot(x, w1_ref[0, half:, :])
    pe = jnp.broadcast_to(pe_ref[0], (8, 2 * half)).astype(BF16)
    pterm = _dot(pe, w1_ref[0])[0:1, :]
    pre = ylo + pltpu.roll(yhi, nb - 1, 0) + pterm
    hcu = pre * pre * pre
    hid = 0.5 * pre * (1.0 + jnp.tanh(0.7978845608028654 * (pre + 0.044715 * hcu)))
    o_ref[0, 0] = _dot(hid.astype(BF16), w2_ref[0]).astype(BF16)


def _compress_call(xkv, pe, w1, w2):
    _, bg, nb, wdt = xkv.shape
    return pl.pallas_call(
        _compress_kernel,
        grid=(2, bg),
        in_specs=[pl.BlockSpec((1, 1, nb, wdt), lambda s, i: (s, i, 0, 0)),
                  pl.BlockSpec((1, 1, CMP_LEN * NSA_HD), lambda s, i: (s, 0, 0)),
                  pl.BlockSpec((1, CMP_LEN * NSA_HD, CMP_HIDDEN), lambda s, i: (s, 0, 0)),
                  pl.BlockSpec((1, CMP_HIDDEN, NSA_HD), lambda s, i: (s, 0, 0))],
        out_specs=pl.BlockSpec((1, 1, nb, NSA_HD), lambda s, i: (s, i, 0, 0)),
        out_shape=jax.ShapeDtypeStruct((2, bg, nb, NSA_HD), BF16),
        compiler_params=_params("parallel", "parallel"),
        name="compress",
    )(xkv, pe, w1, w2)


def _cmpsel_kernel(q_ref, kc_ref, vc_ref, o_ref, sel_ref, *, tq, nsel):
    t0 = pl.program_id(2) * tq
    kc = kc_ref[0, 0]
    vc = vc_ref[0, 0]
    ncp = kc.shape[0]
    tpos = t0 + lax.broadcasted_iota(jnp.int32, (tq, 1), 0)
    cidx = lax.broadcasted_iota(jnp.int32, (1, ncp), 1)
    cmask = (cidx * CMP_STRIDE + (CMP_LEN - 1)) <= tpos
    psum = jnp.zeros((tq, ncp), F32)
    for r in range(NSA_GROUP):
        q = q_ref[:, r * NSA_HD:(r + 1) * NSA_HD]
        s = jnp.where(cmask, _dot_nt(q, kc) * (NSA_HD ** -0.5), NEG)
        m = jnp.max(s, axis=-1, keepdims=True)
        e = jnp.where(cmask, jnp.exp(s - m), 0.0)
        p = e / jnp.maximum(jnp.sum(e, axis=-1, keepdims=True), 1e-30)
        o_ref[:, r * NSA_HD:(r + 1) * NSA_HD] = _dot(p.astype(BF16), vc).astype(BF16)
        psum = psum + p
    ci = lax.broadcasted_iota(jnp.int32, (nsel, ncp), 1) * CMP_STRIDE
    si = lax.broadcasted_iota(jnp.int32, (nsel, ncp), 0) * SEL_BLOCK
    ov = jnp.where((ci < si + SEL_BLOCK) & (ci + CMP_LEN > si), 1.0, 0.0).astype(BF16)
    hi, lo = _split(psum)
    imp = _dot_nt(ov, hi) + _dot_nt(ov, lo)
    tlane = t0 + lax.broadcasted_iota(jnp.int32, (1, tq), 1)
    jblk = lax.broadcasted_iota(jnp.int32, (nsel, 1), 0)
    tblk = tlane // SEL_BLOCK
    forced = (jblk == 0) | (jblk == tblk) | (jblk == tblk - 1)
    valid = (jblk * SEL_BLOCK) <= tlane
    score = jnp.where(forced, jnp.inf, jnp.where(valid, imp, -jnp.inf))
    rank = jnp.zeros((nsel, tq), F32)
    for i in range(nsel):
        si_ = score[i:i + 1, :]
        tie = jnp.where(jblk > i, 1.0, 0.0)
        rank = rank + jnp.where(si_ > score, 1.0, jnp.where(si_ == score, tie, 0.0))
    ntop = min(SEL_TOPK, nsel)
    chosen = jnp.where(valid & (rank < ntop), 1.0, 0.0)
    sel_ref[0, 0] = jnp.transpose(chosen).astype(BF16)


def _cmpsel_call(main, kvc, B, T, tq=256):
    n = main.shape[0]
    nq = T // tq
    nsel = T // SEL_BLOCK
    ncp = kvc.shape[2]
    G = NSA_KV_HEADS
    qw = NSA_GROUP * NSA_HD
    return pl.pallas_call(
        functools.partial(_cmpsel_kernel, tq=tq, nsel=nsel),
        grid=(B, G, nq),
        in_specs=[pl.BlockSpec((tq, qw), lambda b, g, t: (b * nq + t, C_NQ // qw + g)),
                  pl.BlockSpec((1, 1, ncp, NSA_HD), lambda b, g, t: (0, b * G + g, 0, 0)),
                  pl.BlockSpec((1, 1, ncp, NSA_HD), lambda b, g, t: (1, b * G + g, 0, 0))],
        out_specs=[pl.BlockSpec((tq, qw), lambda b, g, t: (b * nq + t, g)),
                   pl.BlockSpec((1, 1, tq, nsel), lambda b, g, t: (b, g, t, 0))],
        out_shape=[jax.ShapeDtypeStruct((n, NSA_HEADS * NSA_HD), BF16),
                   jax.ShapeDtypeStruct((B, G, T, nsel), BF16)],
        compiler_params=_params("parallel", "parallel", "parallel"),
        name="cmpsel",
    )(main, kvc, kvc)


def _nsa_kernel(q_ref, ks_ref, vs_ref, kw_ref, vw_ref, sel_ref, oc_ref, s_ref, o_ref, s_sc, *, tq, tk, T, nsel):
    R = NSA_GROUP
    hd = NSA_HD
    g = pl.program_id(1)
    t0 = pl.program_id(2) * tq
    qscale = (hd ** -0.5) * LOG2E
    q4 = jnp.concatenate([q_ref[:, r * hd:(r + 1) * hd] for r in range(R)], axis=0)
    q4 = (q4.astype(F32) * qscale).astype(BF16)
    rowpos = t0 + lax.broadcasted_iota(jnp.int32, (tq, 1), 0)
    selm = sel_ref[0, 0].astype(F32) - 1.0
    selm4 = jnp.concatenate([selm] * R, axis=0).astype(BF16)

    qa = jnp.concatenate([q4, selm4], axis=1)
    jcol = lax.broadcasted_iota(jnp.int32, (tk, nsel), 1)
    crow = lax.broadcasted_iota(jnp.int32, (tk, nsel), 0) // SEL_BLOCK
    kcol = lax.broadcasted_iota(jnp.int32, (1, tk), 1)

    def scores(kt):
        k0 = pl.multiple_of(kt * tk, tk)
        ind = jnp.where(jcol == crow + kt * (tk // SEL_BLOCK), 1e30, 0.0).astype(BF16)
        s_sc[kt % 2] = _dot_nt(qa, jnp.concatenate([ks_ref[pl.ds(k0, tk), :], ind], axis=1))

    def attend(kt, carry, diagonal):
        m, l, acc = carry
        k0 = pl.multiple_of(kt * tk, tk)
        s = s_sc[kt % 2].reshape(R, tq, tk)
        if diagonal:
            s = s + jnp.where((k0 + kcol) <= rowpos, 0.0, NEG)[None]
        m_new = jnp.maximum(m, jnp.max(s, axis=-1, keepdims=True))
        alpha = jnp.exp2(m - m_new)
        p = jnp.exp2(s - m_new)
        l = alpha * l + jnp.sum(p, axis=-1, keepdims=True)
        pv = _dot(p.reshape(R * tq, tk).astype(BF16), vs_ref[pl.ds(k0, tk), :]).reshape(R, tq, hd)
        return m_new, l, alpha * acc + pv

    def step(kt, carry):
        carry = attend(kt, carry, False)
        scores(kt + 1)
        return carry

    init = (jnp.full((R, tq, 1), NEG, F32), jnp.zeros((R, tq, 1), F32), jnp.zeros((R, tq, hd), F32))
    kdiag = t0 // tk
    scores(0)
    carry = lax.fori_loop(0, kdiag, step, init)
    _, l_s, acc_s = attend(kdiag, carry, True)
    o_sel = acc_s / l_s

    wlen = WIN + tq
    w0 = pl.multiple_of(jnp.maximum(t0 - WIN, 0), tq)
    kw = kw_ref[pl.ds(w0, wlen), :]
    vw = vw_ref[pl.ds(w0, wlen), :]
    kpos = w0 + lax.broadcasted_iota(jnp.int32, (1, wlen), 1)
    wbias = jnp.where((kpos <= rowpos) & (rowpos - kpos < WIN), 0.0, NEG)
    sw = _dot_nt(q4, kw).reshape(R, tq, wlen) + wbias[None]
    mw = jnp.max(sw, axis=-1, keepdims=True)
    pw = jnp.exp2(sw - mw)
    lw = jnp.sum(pw, axis=-1, keepdims=True)
    o_win = _dot(pw.reshape(R * tq, wlen).astype(BF16), vw).reshape(R, tq, hd) / lw

    ng = R * 3
    graw = jnp.where(g == 0, s_ref[:, S_NG:S_NG + ng], s_ref[:, S_NG + ng:S_NG + 2 * ng])
    gates = _sigmoid(graw)
    for r in range(R):
        oc = oc_ref[:, r * hd:(r + 1) * hd].astype(F32)
        out = (gates[:, 3 * r:3 * r + 1] * oc + gates[:, 3 * r + 1:3 * r + 2] * o_sel[r]
               + gates[:, 3 * r + 2:3 * r + 3] * o_win[r])
        o_ref[:, r * hd:(r + 1) * hd] = out.astype(BF16)


def _nsa_call(main, small, sel, o_cmp, B, T, tq=128, tk=512):
    n = main.shape[0]
    nq = T // tq
    nsel = T // SEL_BLOCK
    G = NSA_KV_HEADS
    hd = NSA_HD
    qw = NSA_GROUP * hd
    tk = min(tk, T)
    kvb = C_KV // hd

    def kvspec(slot):
        return pl.BlockSpec((T, hd), lambda b, g, t: (b, kvb + 2 * slot + g))

    return pl.pallas_call(
        functools.partial(_nsa_kernel, tq=tq, tk=tk, T=T, nsel=nsel),
        grid=(B, G, nq),
        in_specs=[pl.BlockSpec((tq, qw), lambda b, g, t: (b * nq + t, C_NQ // qw + g)),
                  kvspec(2), kvspec(3), kvspec(4), kvspec(5),
                  pl.BlockSpec((1, 1, tq, nsel), lambda b, g, t: (b, g, t, 0)),
                  pl.BlockSpec((tq, qw), lambda b, g, t: (b * nq + t, g)),
                  pl.BlockSpec((tq, LANES), lambda b, g, t: (b * nq + t, 0))],
        out_specs=pl.BlockSpec((tq, qw), lambda b, g, t: (b * nq + t, g)),
        out_shape=jax.ShapeDtypeStruct((n, NSA_HEADS * hd), BF16),
        scratch_shapes=[pltpu.VMEM((2, NSA_GROUP * tq, tk), F32)],
        compiler_params=_params("parallel", "parallel", "parallel"),
        name="nsa",
    )(main, main, main, main, main, sel, o_cmp, small)


def _merge_kernel(yg_ref, yn_ref, mg_ref, x_ref, g1_ref, sc_ref, sh_ref, n2_ref, pg_ref, pn_ref, wo_ref,
                  rw_ref, rb_ref, h_ref, u_ref, rt_ref, cnt_ref, carry_sc, *, tm):
    d = D_MODEL

    @pl.when(pl.program_id(0) == 0)
    def _():
        carry_sc[...] = jnp.zeros_like(carry_sc)

    a = _dot(yg_ref[...], pg_ref[...])
    b = _dot(yn_ref[...], pn_ref[...])
    mg = _sigmoid(mg_ref[...].astype(F32))
    merged = mg[:, :d] * a + mg[:, d:] * b
    mix = _dot(merged.astype(BF16), wo_ref[...])
    h = x_ref[...] + g1_ref[0] * mix
    h_ref[...] = h
    y = h * lax.rsqrt(jnp.mean(h * h, axis=-1, keepdims=True) + RMS_EPS) * n2_ref[...]
    u = y * (1.0 + sc_ref[0]) + sh_ref[0]
    u_ref[...] = u

    lane = lax.broadcasted_iota(jnp.int32, (1, LANES), 1).astype(F32)
    logits = jnp.where(lane < N_EXPERTS, _dot3(u, rw_ref[...]) + rb_ref[...], -jnp.inf)
    vals, idxs = [], []
    cur = logits
    for _ in range(TOP_K):
        m = jnp.max(cur, axis=-1, keepdims=True)
        idx = jnp.min(jnp.where(cur == m, lane, float(LANES)), axis=-1, keepdims=True)
        vals.append(m)
        idxs.append(idx)
        cur = jnp.where(lane == idx, -jnp.inf, cur)
    es = [jnp.exp(v - vals[0]) for v in vals]
    den = es[0] + es[1] + es[2] + es[3]
    onehots = [lane == idx for idx in idxs]
    oh = jnp.zeros((tm, LANES), F32)
    for o in onehots:
        oh = oh + jnp.where(o, 1.0, 0.0)
    row = lax.broadcasted_iota(jnp.int32, (tm, tm), 0)
    col = lax.broadcasted_iota(jnp.int32, (tm, tm), 1)
    tril = jnp.where(col <= row, 1.0, 0.0).astype(BF16)
    cnt = _dot(tril, oh.astype(BF16))
    tot = cnt + carry_sc[0:1, :]
    route = jnp.zeros((tm, LANES), F32)
    for k in range(TOP_K):
        pos = jnp.sum(jnp.where(onehots[k], tot - 1.0, 0.0), axis=-1, keepdims=True)
        route = jnp.where(lane == k, es[k] / den, route)
        route = jnp.where(lane == TOP_K + k, idxs[k], route)
        route = jnp.where(lane == 2 * TOP_K + k, pos, route)
    rt_ref[...] = route
    newc = tot[tm - 1:tm, :]
    carry_sc[...] = jnp.broadcast_to(newc, carry_sc.shape)
    cnt_ref[...] = jnp.broadcast_to(newc, cnt_ref.shape)


def _merge_call(y_gla, y_nsa, main, x2, gate1, scale2, shift2, n2g, pg, pn, wo, rw, rb, T, tm=512):
    n, d = x2.shape
    tpb = T // tm
    bmap = lambda i: (i // tpb, 0, 0)
    full = lambda i: (0, 0)
    return pl.pallas_call(
        functools.partial(_merge_kernel, tm=tm),
        grid=(n // tm,),
        in_specs=[pl.BlockSpec((tm, d), lambda i: (i, 0)),
                  pl.BlockSpec((tm, d), lambda i: (i, 0)),
                  pl.BlockSpec((tm, 2 * d), lambda i: (i, C_MG // (2 * d))),
                  pl.BlockSpec((tm, d), lambda i: (i, 0)),
                  pl.BlockSpec((1, 1, d), bmap), pl.BlockSpec((1, 1, d), bmap), pl.BlockSpec((1, 1, d), bmap),
                  pl.BlockSpec((1, d), full),
                  pl.BlockSpec((d, d), full), pl.BlockSpec((d, d), full), pl.BlockSpec((d, d), full),
                  pl.BlockSpec((d, LANES), full), pl.BlockSpec((1, LANES), full)],
        out_specs=[pl.BlockSpec((tm, d), lambda i: (i, 0)),
                   pl.BlockSpec((tm, d), lambda i: (i, 0)),
                   pl.BlockSpec((tm, LANES), lambda i: (i, 0)),
                   pl.BlockSpec((8, LANES), full)],
        out_shape=[jax.ShapeDtypeStruct((n, d), F32), jax.ShapeDtypeStruct((n, d), F32),
                   jax.ShapeDtypeStruct((n, LANES), F32), jax.ShapeDtypeStruct((8, LANES), F32)],
        scratch_shapes=[pltpu.VMEM((8, LANES), F32)],
        compiler_params=_params("arbitrary"),
        name="merge",
    )(y_gla, y_nsa, main, x2, gate1, scale2, shift2, n2g, pg, pn, wo, rw, rb)


SUBLANES = 8
ROW_TILES = D_MODEL // LANES
assert ROW_TILES == SUBLANES


def _to_token_tiles(dst_ref, src, rows):
    for j in range(ROW_TILES):
        dst_ref[pl.ds(j, rows, stride=SUBLANES), :] = src[:, j * LANES:(j + 1) * LANES]


def _from_token_tiles(src_ref, rows):
    return [src_ref[pl.ds(j, rows, stride=SUBLANES), :] for j in range(ROW_TILES)]


def _dispatch_kernel(dest_ref, pend_ref, u_ref, xs_ref, xt, zbuf, sem, zsem, *, td):
    base = pl.program_id(0) * (td * TOP_K)
    blk = MOE_ROWS * SUBLANES

    @pl.when(pl.program_id(0) == 0)
    def _():
        zbuf[...] = jnp.zeros_like(zbuf)

        def last_block(e):
            end = pend_ref[e]
            prev = pend_ref[e - 1] if e else 0
            start = pl.multiple_of(jnp.maximum(end - MOE_ROWS, 0) * SUBLANES, blk)
            return end > prev, pltpu.make_async_copy(zbuf, xs_ref.at[pl.ds(start, blk)], zsem)

        def unused_block(b):
            return b * MOE_ROWS >= pend_ref[N_EXPERTS - 1], pltpu.make_async_copy(
                zbuf, xs_ref.at[pl.ds(b * blk, blk)], zsem)

        nblk = xs_ref.shape[0] // blk
        fills = [functools.partial(last_block, e) for e in range(N_EXPERTS)]
        fills += [functools.partial(unused_block, b) for b in range(nblk - N_EXPERTS, nblk)]
        for f in fills:
            cond, fill = f()
            pl.when(cond)(fill.start)
        for f in fills:
            cond, fill = f()
            pl.when(cond)(fill.wait)

    _to_token_tiles(xt, u_ref[...], td)

    def row_copy(r, slot):
        return pltpu.make_async_copy(xt.at[pl.ds(pl.multiple_of(r * SUBLANES, SUBLANES), SUBLANES)],
                                     xs_ref.at[pl.ds(pl.multiple_of(slot * SUBLANES, SUBLANES), SUBLANES)], sem)

    def issue(r, c):
        for k in range(TOP_K):
            row_copy(r, dest_ref[base + r * TOP_K + k]).start()
        return c

    def drain(r, c):
        for k in range(TOP_K):
            row_copy(r, dest_ref[base + r * TOP_K + k]).wait()
        return c

    lax.fori_loop(0, td, issue, 0)
    lax.fori_loop(0, td, drain, 0)


def _dispatch_call(dest, pends, u2, p_rows, td=256):
    n, d = u2.shape
    return pl.pallas_call(
        functools.partial(_dispatch_kernel, td=td),
        grid_spec=pltpu.PrefetchScalarGridSpec(
            num_scalar_prefetch=2,
            grid=(n // td,),
            in_specs=[pl.BlockSpec((td, d), lambda i, dest, pends: (i, 0))],
            out_specs=pl.BlockSpec(memory_space=pl.ANY),
            scratch_shapes=[pltpu.VMEM((td * SUBLANES, LANES), F32), pltpu.VMEM((MOE_ROWS * SUBLANES, LANES), F32),
                            pltpu.SemaphoreType.DMA(()), pltpu.SemaphoreType.DMA(())]),
        out_shape=jax.ShapeDtypeStruct((p_rows * SUBLANES, LANES), F32),
        compiler_params=_params("arbitrary"),
        name="dispatch",
    )(dest, pends, u2)


def _moe_kernel(be_ref, na_ref, x_ref, w1_ref, b1_ref, w2_ref, b2_ref, y_ref, w1b, w2b):
    i = pl.program_id(0)
    e = be_ref[i]
    prev = be_ref[jnp.maximum(i - 1, 0)]
    active = i < na_ref[0]
    f = D_EXPERT
    m = MOE_ROWS

    @pl.when(active & ((i == 0) | (e != prev)))
    def _():
        w1b[...] = w1_ref[0].astype(BF16)
        w2b[...] = w2_ref[0].astype(BF16)

    @pl.when(active)
    def _():
        x = jnp.concatenate([t.astype(BF16) for t in _from_token_tiles(x_ref, m)], axis=1)
        hb = _dot(x, w1b[...]) + b1_ref[0]
        gate = jnp.minimum(hb[:, :f], SWIGLU_LIMIT)
        up = jnp.clip(hb[:, f:], -SWIGLU_LIMIT, SWIGLU_LIMIT)
        act = gate * _sigmoid(SWIGLU_ALPHA * gate) * (up + 1.0)
        _to_token_tiles(y_ref, _dot(act.astype(BF16), w2b[...]) + b2_ref[0], m)

    @pl.when(jnp.logical_not(active))
    def _():
        y_ref[...] = jnp.zeros_like(y_ref)


def _moe_call(blk_e, nact, xs, w1, b1, w2, b2):
    d, f2 = w1.shape[1], w1.shape[2]
    blk = MOE_ROWS * SUBLANES
    return pl.pallas_call(
        _moe_kernel,
        grid_spec=pltpu.PrefetchScalarGridSpec(
            num_scalar_prefetch=2,
            grid=(xs.shape[0] // blk,),
            in_specs=[pl.BlockSpec((blk, LANES), lambda i, be, na: (i, 0)),
                      pl.BlockSpec((1, d, f2), lambda i, be, na: (be[i], 0, 0)),
                      pl.BlockSpec((1, 1, f2), lambda i, be, na: (be[i], 0, 0)),
                      pl.BlockSpec((1, f2 // 2, d), lambda i, be, na: (be[i], 0, 0)),
                      pl.BlockSpec((1, 1, d), lambda i, be, na: (be[i], 0, 0))],
            out_specs=pl.BlockSpec((blk, LANES), lambda i, be, na: (i, 0)),
            scratch_shapes=[pltpu.VMEM((d, f2), BF16), pltpu.VMEM((f2 // 2, d), BF16)]),
        out_shape=jax.ShapeDtypeStruct(xs.shape, F32),
        compiler_params=_params("arbitrary"),
        name="moe",
    )(blk_e, nact, xs, w1, b1, w2, b2)


def _combine_kernel(dest_ref, h_ref, rt_ref, g2_ref, fg_ref, ys_ref, o_ref, buf, sems, *, tc):
    i = pl.program_id(0)
    slot = i % 2

    def row_copy(step, sl, r, k):
        src = dest_ref[step * (tc * TOP_K) + r * TOP_K + k]
        return pltpu.make_async_copy(
            ys_ref.at[pl.ds(pl.multiple_of(src * SUBLANES, SUBLANES), SUBLANES)],
            buf.at[sl, k, pl.ds(pl.multiple_of(r * SUBLANES, SUBLANES), SUBLANES)], sems.at[sl])

    def issue(step, sl):
        def body(r, c):
            for k in range(TOP_K):
                row_copy(step, sl, r, k).start()
            return c
        lax.fori_loop(0, tc, body, 0)

    def drain(step, sl):
        def body(r, c):
            for k in range(TOP_K):
                row_copy(step, sl, r, k).wait()
            return c
        lax.fori_loop(0, tc, body, 0)

    @pl.when(i == 0)
    def _():
        issue(0, 0)

    @pl.when(i + 1 < pl.num_programs(0))
    def _():
        issue(i + 1, 1 - slot)

    drain(i, slot)
    rt = rt_ref[...]
    slabs = [_from_token_tiles(buf.at[slot, k], tc) for k in range(TOP_K)]
    cols = []
    for j in range(ROW_TILES):
        col = rt[:, 0:1] * slabs[0][j]
        for k in range(1, TOP_K):
            col = col + rt[:, k:k + 1] * slabs[k][j]
        cols.append(col)
    moe = jnp.concatenate(cols, axis=1)
    hh = h_ref[...] + g2_ref[0] * moe
    o_ref[...] = hh * lax.rsqrt(jnp.mean(hh * hh, axis=-1, keepdims=True) + RMS_EPS) * fg_ref[...]


def _combine_call(dest, h, route, gate2, fg, ys, T, tc=128):
    n, d = h.shape
    tpb = T // tc
    return pl.pallas_call(
        functools.partial(_combine_kernel, tc=tc),
        grid_spec=pltpu.PrefetchScalarGridSpec(
            num_scalar_prefetch=1,
            grid=(n // tc,),
            in_specs=[pl.BlockSpec((tc, d), lambda i, dest: (i, 0)),
                      pl.BlockSpec((tc, LANES), lambda i, dest: (i, 0)),
                      pl.BlockSpec((1, 1, d), lambda i, dest: (i // tpb, 0, 0)),
                      pl.BlockSpec((1, d), lambda i, dest: (0, 0)),
                      pl.BlockSpec(memory_space=pl.ANY)],
            out_specs=pl.BlockSpec((tc, d), lambda i, dest: (i, 0)),
            scratch_shapes=[pltpu.VMEM((2, TOP_K, tc * SUBLANES, LANES), F32), pltpu.SemaphoreType.DMA((2,))]),
        out_shape=jax.ShapeDtypeStruct((n, d), F32),
        compiler_params=_params("arbitrary"),
        name="combine",
    )(dest, h, route, gate2, fg, ys)


def _mixer(x2, mod, norm1_g, w_in, gla_wa2, gla_ba, gla_norm_g, cmp_pe, cmp_w1, cmp_w2, B, T):
    d = D_MODEL
    G = NSA_KV_HEADS
    hd = NSA_HD
    shift1 = mod[:, 0 * d:1 * d].reshape(B, 1, d)
    scale1 = mod[:, 1 * d:2 * d].reshape(B, 1, d)
    w_main = jnp.concatenate([w_in[:, :_O_GA], w_in[:, _O_NQ:_O_KV], w_in[:, _O_MG:_O_END],
                              w_in[:, _O_KV:_O_NG]], axis=1).astype(BF16)
    w_small = jnp.concatenate([w_in[:, _O_GA:_O_NQ], w_in[:, _O_NG:_O_MG],
                               jnp.zeros((d, LANES - GLA_GATE_RANK - NSA_HEADS * 3), F32)], axis=1).astype(BF16)
    main, small = _inproj_call(x2, norm1_g.reshape(1, d), scale1, shift1, w_main, w_small, T)
    y_gla = _gla_call(main, small, gla_wa2, gla_ba.reshape(1, -1), gla_norm_g.reshape(1, -1), B, T)
    kv = main[:, C_KV:C_KV + 2 * G * hd].reshape(B, T // CMP_STRIDE, CMP_STRIDE, 2, G, hd)
    xkv = kv.transpose(3, 0, 4, 1, 2, 5).reshape(2, B * G, T // CMP_STRIDE, CMP_STRIDE * hd)
    kvc = _compress_call(xkv, cmp_pe.reshape(2, 1, CMP_LEN * hd), cmp_w1.astype(BF16), cmp_w2.astype(BF16))
    o_cmp, sel = _cmpsel_call(main, kvc, B, T)
    y_nsa = _nsa_call(main, small, sel, o_cmp, B, T)
    return main, y_gla, y_nsa


def _route_tables(route, counts, n):
    m = MOE_ROWS
    e = route[:, TOP_K:2 * TOP_K].astype(jnp.int32)
    pos = route[:, 2 * TOP_K:3 * TOP_K].astype(jnp.int32)
    cnt = counts[0, :N_EXPERTS].astype(jnp.int32)
    padded = ((cnt + m - 1) // m) * m
    pends = jnp.cumsum(padded)
    pstarts = pends - padded
    dest = (pstarts[e] + pos).reshape(-1)
    p_rows = n * TOP_K + N_EXPERTS * m
    nblk = p_rows // m
    starts = jnp.arange(nblk, dtype=jnp.int32) * m
    blk_e = jnp.minimum(jnp.sum((pends[None, :] <= starts[:, None]).astype(jnp.int32), axis=1), N_EXPERTS - 1)
    nact = (pends[-1:] // m).astype(jnp.int32)
    return dest, pends.astype(jnp.int32), blk_e, nact, p_rows


def kernel(x, c, w_ada, b_ada, norm1_g, w_in, gla_wa2, gla_ba, gla_norm_g, cmp_pe, cmp_w1, cmp_w2, proj_gla,
           proj_nsa, w_out, norm2_g, router_w, router_b, moe_w1, moe_b1, moe_w2, moe_b2, final_g):
    B, T, d = x.shape
    n = B * T
    assert w_ada.shape[0] == 1 and d == D_MODEL and B <= 8, "single-layer block of width D_MODEL"
    l = 0
    h = x.reshape(n, d)
    c8 = jnp.concatenate([c, jnp.zeros((8 - B, d), c.dtype)], axis=0)
    mod = _mod_call(c8, w_ada[l], b_ada[l].reshape(1, -1))[:B]
    gate1 = mod[:, 2 * d:3 * d].reshape(B, 1, d)
    shift2 = mod[:, 3 * d:4 * d].reshape(B, 1, d)
    scale2 = mod[:, 4 * d:5 * d].reshape(B, 1, d)
    gate2 = mod[:, 5 * d:6 * d].reshape(B, 1, d)
    main, y_gla, y_nsa = _mixer(h, mod, norm1_g[l], w_in[l], gla_wa2[l], gla_ba[l], gla_norm_g[l],
                                cmp_pe[l], cmp_w1[l], cmp_w2[l], B, T)
    rw = jnp.concatenate([router_w[l], jnp.zeros((d, LANES - N_EXPERTS), F32)], axis=1)
    rb = jnp.concatenate([router_b[l], jnp.zeros((LANES - N_EXPERTS,), F32)]).reshape(1, LANES)
    h1, u2, route, counts = _merge_call(
        y_gla, y_nsa, main, h, gate1, scale2, shift2, norm2_g[l].reshape(1, d),
        proj_gla[l].astype(BF16), proj_nsa[l].astype(BF16), w_out[l].astype(BF16), rw, rb, T)
    dest, pends, blk_e, nact, p_rows = _route_tables(route, counts, n)
    xs = _dispatch_call(dest, pends, u2, p_rows)
    ys = _moe_call(blk_e, nact, xs, moe_w1[l], moe_b1[l].reshape(N_EXPERTS, 1, -1), moe_w2[l],
                   moe_b2[l].reshape(N_EXPERTS, 1, -1))
    out = _combine_call(dest, h1, route, gate2, final_g.reshape(1, d), ys, T)
    return out.reshape(B, T, d)
```

```python
import functools

import jax
import jax.numpy as jnp
from jax import lax
from jax.experimental import pallas as pl
from jax.experimental.pallas import tpu as pltpu

F32 = jnp.float32
BF16 = jnp.bfloat16

D_MODEL = 1024
GLA_HEADS = 4
GLA_DK = 128
GLA_DV = 256
GLA_GATE_RANK = 16
GLA_TAU = 16.0
GLA_CHUNK = 64
NSA_HEADS = 8
NSA_KV_HEADS = 2
NSA_HD = 128
NSA_GROUP = NSA_HEADS // NSA_KV_HEADS
CMP_LEN = 32
CMP_STRIDE = 16
CMP_HIDDEN = 2 * NSA_HD
SEL_BLOCK = 64
SEL_TOPK = 16
WIN = 512
N_EXPERTS = 32
TOP_K = 4
D_EXPERT = D_MODEL
SWIGLU_LIMIT = 7.0
SWIGLU_ALPHA = 1.702
RMS_EPS = 1e-5

LANES = 128
NEG = -1e30
LOG2E = 1.4426950408889634
MOE_ROWS = 512
VMEM_LIMIT = 56 * 1024 * 1024

C_GQ, C_GK, C_GV, C_GR, C_NQ, C_MG, C_KV = 0, 512, 1024, 2048, 3072, 4096, 6144
MAIN_COLS = 7680
CMP_KV_COLS = 2 * NSA_KV_HEADS * NSA_HD
S_GA, S_NG = 0, 16

_O_GA = 3072
_O_NQ = 3088
_O_KV = 4112
_O_NG = 5648
_O_MG = 5672
_O_END = 7720


def _dot(a, b):
    return jnp.dot(a, b, preferred_element_type=F32)


def _dot_nt(a, b):
    return lax.dot_general(a, b, (((1,), (1,)), ((), ())), preferred_element_type=F32)


def _dot_tn(a, b):
    return lax.dot_general(a, b, (((0,), (0,)), ((), ())), preferred_element_type=F32)


def _split(a):
    hi = a.astype(BF16)
    lo = (a - hi.astype(F32)).astype(BF16)
    return hi, lo


def _dot3(a, b):
    ah, al = _split(a)
    bh, bl = _split(b)
    return _dot(ah, bh) + _dot(ah, bl) + _dot(al, bh)


def _sigmoid(x):
    return 1.0 / (1.0 + jnp.exp(-x))


def _params(*sem):
    return pltpu.CompilerParams(dimension_semantics=sem, vmem_limit_bytes=VMEM_LIMIT)


def _mod_kernel(c_ref, w_ref, b_ref, o_ref):
    c = c_ref[...]
    cond = c * _sigmoid(c)
    o_ref[...] = _dot3(cond, w_ref[...]) + b_ref[...]


def _mod_call(c8, w_ada, b_ada):
    d = c8.shape[1]
    n = w_ada.shape[1]
    return pl.pallas_call(
        _mod_kernel,
        grid=(n // d,),
        in_specs=[pl.BlockSpec((8, d), lambda j: (0, 0)),
                  pl.BlockSpec((d, d), lambda j: (0, j)),
                  pl.BlockSpec((1, d), lambda j: (0, j))],
        out_specs=pl.BlockSpec((8, d), lambda j: (0, j)),
        out_shape=jax.ShapeDtypeStruct((8, n), F32),
        compiler_params=_params("parallel"),
        name="mod",
    )(c8, w_ada, b_ada)


def _inproj_kernel(x_ref, g_ref, sc_ref, sh_ref, w_ref, ws_ref, o_ref, s_ref, kv_ref, u_sc, *, kv_off):
    @pl.when(pl.program_id(1) == 0)
    def _():
        xf = x_ref[...]
        ms = jnp.mean(xf * xf, axis=-1, keepdims=True)
        y = xf * lax.rsqrt(ms + RMS_EPS) * g_ref[...]
        u = (y * (1.0 + sc_ref[0]) + sh_ref[0]).astype(BF16)
        u_sc[...] = u
        s_ref[...] = _dot(u, ws_ref[...])

    acc = _dot(u_sc[...], w_ref[...])
    o_ref[...] = acc.astype(BF16)

    @pl.when(pl.program_id(1) == pl.num_programs(1) - 1)
    def _():
        kv_ref[...] = acc[:, kv_off:kv_off + CMP_KV_COLS]


def _inproj_call(x2, g, scale, shift, w_main, w_small, T, tm=1024, tn=1920):
    n, d = x2.shape
    tpb = T // tm
    ncol = MAIN_COLS // tn
    kv_off = C_KV - (ncol - 1) * tn
    assert kv_off >= 0 and kv_off % LANES == 0 and kv_off + CMP_KV_COLS <= tn
    return pl.pallas_call(
        functools.partial(_inproj_kernel, kv_off=kv_off),
        grid=(n // tm, ncol),
        in_specs=[pl.BlockSpec((tm, d), lambda i, j: (i, 0)),
                  pl.BlockSpec((1, d), lambda i, j: (0, 0)),
                  pl.BlockSpec((1, 1, d), lambda i, j: (i // tpb, 0, 0)),
                  pl.BlockSpec((1, 1, d), lambda i, j: (i // tpb, 0, 0)),
                  pl.BlockSpec((d, tn), lambda i, j: (0, j)),
                  pl.BlockSpec((d, LANES), lambda i, j: (0, 0))],
        out_specs=[pl.BlockSpec((tm, tn), lambda i, j: (i, j)),
                   pl.BlockSpec((tm, LANES), lambda i, j: (i, 0)),
                   pl.BlockSpec((tm, CMP_KV_COLS), lambda i, j: (i, 0))],
        out_shape=[jax.ShapeDtypeStruct((n, MAIN_COLS), BF16),
                   jax.ShapeDtypeStruct((n, LANES), F32),
                   jax.ShapeDtypeStruct((n, CMP_KV_COLS), F32)],
        scratch_shapes=[pltpu.VMEM((tm, d), BF16)],
        compiler_params=_params("parallel", "arbitrary"),
        name="inproj",
    )(x2, g, scale, shift, w_main, w_small)


def _gla_kernel(q_ref, k_ref, v_ref, r_ref, s_ref, wa_ref, ba_ref, g_ref, o_ref, st_sc, *, tt):
    @pl.when(pl.program_id(1) == 0)
    def _():
        st_sc[...] = jnp.zeros_like(st_sc)

    C = GLA_CHUNK
    nc = tt // C
    dk, dv = GLA_DK, GLA_DV
    ga = s_ref[:, S_GA:S_GA + GLA_GATE_RANK]
    z = _dot3(ga, wa_ref[...]) + ba_ref[...]
    la = (jnp.minimum(z, 0.0) - jnp.log(1.0 + jnp.exp(-jnp.abs(z)))) * (1.0 / GLA_TAU)
    row = lax.broadcasted_iota(jnp.int32, (tt, tt), 0)
    col = lax.broadcasted_iota(jnp.int32, (tt, tt), 1)
    causal = ((row // C) == (col // C)) & (col <= row)
    tril = jnp.where(causal, 1.0, 0.0).astype(BF16)
    hi, lo = _split(la)
    bc_all = _dot(tril, hi) + _dot(tril, lo)
    for h in range(GLA_HEADS):
        bc = bc_all[:, h * dk:(h + 1) * dk]
        bls = [bc[(c + 1) * C - 1:(c + 1) * C, :] for c in range(nc)]
        blr = jnp.concatenate([jnp.broadcast_to(b, (C, dk)) for b in bls], axis=0)
        q = q_ref[:, h * dk:(h + 1) * dk].astype(F32) * (dk ** -0.5)
        k = k_ref[:, h * dk:(h + 1) * dk].astype(F32)
        v = v_ref[:, h * dv:(h + 1) * dv]
        qe = (q * jnp.exp(bc)).astype(BF16)
        ke = (k * jnp.exp(-bc)).astype(BF16)
        kd = (k * jnp.exp(blr - bc)).astype(BF16)
        a = jnp.where(causal, _dot_nt(qe, ke), 0.0)
        intra = _dot(a.astype(BF16), v)
        st = st_sc[h]
        inter = []
        for c in range(nc):
            sl = slice(c * C, (c + 1) * C)
            inter.append(_dot_nt(qe[sl], st.astype(BF16)))
            st = st * jnp.exp(bls[c]) + _dot_tn(v[sl], kd[sl])
        st_sc[h] = st
        o = intra + jnp.concatenate(inter, axis=0)
        y = o * lax.rsqrt(jnp.mean(o * o, axis=-1, keepdims=True) + RMS_EPS) * g_ref[...]
        r = r_ref[:, h * dv:(h + 1) * dv].astype(F32)
        o_ref[:, h * dv:(h + 1) * dv] = (y * (r * _sigmoid(r))).astype(BF16)


def _gla_call(main, small, wa2, ba, norm_g, B, T, tt=256):
    n = main.shape[0]
    nt = T // tt
    H = GLA_HEADS
    rowmap = lambda b, t: b * nt + t
    return pl.pallas_call(
        functools.partial(_gla_kernel, tt=tt),
        grid=(B, nt),
        in_specs=[pl.BlockSpec((tt, H * GLA_DK), lambda b, t: (rowmap(b, t), C_GQ // (H * GLA_DK))),
                  pl.BlockSpec((tt, H * GLA_DK), lambda b, t: (rowmap(b, t), C_GK // (H * GLA_DK))),
                  pl.BlockSpec((tt, H * GLA_DV), lambda b, t: (rowmap(b, t), C_GV // (H * GLA_DV))),
                  pl.BlockSpec((tt, H * GLA_DV), lambda b, t: (rowmap(b, t), C_GR // (H * GLA_DV))),
                  pl.BlockSpec((tt, LANES), lambda b, t: (rowmap(b, t), 0)),
                  pl.BlockSpec((GLA_GATE_RANK, H * GLA_DK), lambda b, t: (0, 0)),
                  pl.BlockSpec((1, H * GLA_DK), lambda b, t: (0, 0)),
                  pl.BlockSpec((1, GLA_DV), lambda b, t: (0, 0))],
        out_specs=pl.BlockSpec((tt, H * GLA_DV), lambda b, t: (rowmap(b, t), 0)),
        out_shape=jax.ShapeDtypeStruct((n, H * GLA_DV), BF16),
        scratch_shapes=[pltpu.VMEM((H, GLA_DV, GLA_DK), F32)],
        compiler_params=_params("parallel", "arbitrary"),
        name="gla",
    )(main, main, main, main, small, wa2, ba, norm_g)


def _compress_kernel(x_ref, pe_ref, w1_ref, w2_ref, o_ref):
    half = CMP_STRIDE * NSA_HD
    nb = x_ref.shape[0] // CMP_STRIDE
    x = jnp.concatenate([x_ref[pl.ds(t, nb, stride=CMP_STRIDE), :].astype(BF16) for t in range(CMP_STRIDE)], axis=1)
    ylo = _dot(x, w1_ref[0, :half, :])
    yhi = _dot(x, w1_ref[0, half:, :])
    pe = jnp.broadcast_to(pe_ref[0], (8, 2 * half)).astype(BF16)
    pterm = _dot(pe, w1_ref[0])[0:1, :]
    pre = ylo + pltpu.roll(yhi, nb - 1, 0) + pterm
    hcu = pre * pre * pre
    hid = 0.5 * pre * (1.0 + jnp.tanh(0.7978845608028654 * (pre + 0.044715 * hcu)))
    o_ref[0, 0] = _dot(hid.astype(BF16), w2_ref[0]).astype(BF16)


def _compress_call(kv32, pe, w1, w2, B, T):
    G = NSA_KV_HEADS
    bg = B * G
    nb = T // CMP_STRIDE
    return pl.pallas_call(
        _compress_kernel,
        grid=(2, bg),
        in_specs=[pl.BlockSpec((T, NSA_HD), lambda s, i: (i // G, s * G + i % G)),
                  pl.BlockSpec((1, 1, CMP_LEN * NSA_HD), lambda s, i: (s, 0, 0)),
                  pl.BlockSpec((1, CMP_LEN * NSA_HD, CMP_HIDDEN), lambda s, i: (s, 0, 0)),
                  pl.BlockSpec((1, CMP_HIDDEN, NSA_HD), lambda s, i: (s, 0, 0))],
        out_specs=pl.BlockSpec((1, 1, nb, NSA_HD), lambda s, i: (s, i, 0, 0)),
        out_shape=jax.ShapeDtypeStruct((2, bg, nb, NSA_HD), BF16),
        compiler_params=_params("parallel", "parallel"),
        name="compress",
    )(kv32, pe, w1, w2)


def _cmpsel_kernel(q_ref, kc_ref, vc_ref, o_ref, sel_ref, *, tq, nsel):
    t0 = pl.program_id(2) * tq
    kc = kc_ref[0, 0]
    vc = vc_ref[0, 0]
    ncp = kc.shape[0]
    tpos = t0 + lax.broadcasted_iota(jnp.int32, (tq, 1), 0)
    cidx = lax.broadcasted_iota(jnp.int32, (1, ncp), 1)
    cmask = (cidx * CMP_STRIDE + (CMP_LEN - 1)) <= tpos
    psum = jnp.zeros((tq, ncp), F32)
    for r in range(NSA_GROUP):
        q = q_ref[:, r * NSA_HD:(r + 1) * NSA_HD]
        s = jnp.where(cmask, _dot_nt(q, kc) * (NSA_HD ** -0.5), NEG)
        m = jnp.max(s, axis=-1, keepdims=True)
        e = jnp.where(cmask, jnp.exp(s - m), 0.0)
        p = e / jnp.maximum(jnp.sum(e, axis=-1, keepdims=True), 1e-30)
        o_ref[:, r * NSA_HD:(r + 1) * NSA_HD] = _dot(p.astype(BF16), vc).astype(BF16)
        psum = psum + p
    ci = lax.broadcasted_iota(jnp.int32, (nsel, ncp), 1) * CMP_STRIDE
    si = lax.broadcasted_iota(jnp.int32, (nsel, ncp), 0) * SEL_BLOCK
    ov = jnp.where((ci < si + SEL_BLOCK) & (ci + CMP_LEN > si), 1.0, 0.0).astype(BF16)
    hi, lo = _split(psum)
    imp = _dot_nt(ov, hi) + _dot_nt(ov, lo)
    tlane = t0 + lax.broadcasted_iota(jnp.int32, (1, tq), 1)
    jblk = lax.broadcasted_iota(jnp.int32, (nsel, 1), 0)
    tblk = tlane // SEL_BLOCK
    forced = (jblk == 0) | (jblk == tblk) | (jblk == tblk - 1)
    valid = (jblk * SEL_BLOCK) <= tlane
    score = jnp.where(forced, jnp.inf, jnp.where(valid, imp, -jnp.inf))
    rank = jnp.zeros((nsel, tq), F32)
    for i in range(nsel):
        si_ = score[i:i + 1, :]
        tie = jnp.where(jblk > i, 1.0, 0.0)
        rank = rank + jnp.where(si_ > score, 1.0, jnp.where(si_ == score, tie, 0.0))
    ntop = min(SEL_TOPK, nsel)
    chosen = jnp.where(valid & (rank < ntop), 1.0, 0.0)
    sel_ref[0, 0] = jnp.transpose(chosen).astype(BF16)


def _cmpsel_call(main, kvc, B, T, tq=256):
    n = main.shape[0]
    nq = T // tq
    nsel = T // SEL_BLOCK
    ncp = kvc.shape[2]
    G = NSA_KV_HEADS
    qw = NSA_GROUP * NSA_HD
    return pl.pallas_call(
        functools.partial(_cmpsel_kernel, tq=tq, nsel=nsel),
        grid=(B, G, nq),
        in_specs=[pl.BlockSpec((tq, qw), lambda b, g, t: (b * nq + t, C_NQ // qw + g)),
                  pl.BlockSpec((1, 1, ncp, NSA_HD), lambda b, g, t: (0, b * G + g, 0, 0)),
                  pl.BlockSpec((1, 1, ncp, NSA_HD), lambda b, g, t: (1, b * G + g, 0, 0))],
        out_specs=[pl.BlockSpec((tq, qw), lambda b, g, t: (b * nq + t, g)),
                   pl.BlockSpec((1, 1, tq, nsel), lambda b, g, t: (b, g, t, 0))],
        out_shape=[jax.ShapeDtypeStruct((n, NSA_HEADS * NSA_HD), BF16),
                   jax.ShapeDtypeStruct((B, G, T, nsel), BF16)],
        compiler_params=_params("parallel", "parallel", "parallel"),
        name="cmpsel",
    )(main, kvc, kvc)


def _nsa_kernel(q_ref, ks_ref, vs_ref, kw_ref, vw_ref, sel_ref, oc_ref, s_ref, o_ref, s_sc, *, tq, tk, T, nsel):
    R = NSA_GROUP
    hd = NSA_HD
    g = pl.program_id(1)
    t0 = pl.program_id(2) * tq
    qscale = (hd ** -0.5) * LOG2E
    q4 = jnp.concatenate([q_ref[:, r * hd:(r + 1) * hd] for r in range(R)], axis=0)
    q4 = (q4.astype(F32) * qscale).astype(BF16)
    rowpos = t0 + lax.broadcasted_iota(jnp.int32, (tq, 1), 0)
    selm = sel_ref[0, 0].astype(F32) - 1.0
    selm4 = jnp.concatenate([selm] * R, axis=0).astype(BF16)

    qa = jnp.concatenate([q4, selm4], axis=1)
    jcol = lax.broadcasted_iota(jnp.int32, (tk, nsel), 1)
    crow = lax.broadcasted_iota(jnp.int32, (tk, nsel), 0) // SEL_BLOCK
    kcol = lax.broadcasted_iota(jnp.int32, (1, tk), 1)

    def scores(kt):
        k0 = pl.multiple_of(kt * tk, tk)
        ind = jnp.where(jcol == crow + kt * (tk // SEL_BLOCK), 1e30, 0.0).astype(BF16)
        s_sc[kt % 2] = _dot_nt(qa, jnp.concatenate([ks_ref[pl.ds(k0, tk), :], ind], axis=1))

    def attend(kt, carry, diagonal):
        m, l, acc = carry
        k0 = pl.multiple_of(kt * tk, tk)
        s = s_sc[kt % 2].reshape(R, tq, tk)
        if diagonal:
            s = s + jnp.where((k0 + kcol) <= rowpos, 0.0, NEG)[None]
        m_new = jnp.maximum(m, jnp.max(s, axis=-1, keepdims=True))
        alpha = jnp.exp2(m - m_new)
        p = jnp.exp2(s - m_new)
        l = alpha * l + jnp.sum(p, axis=-1, keepdims=True)
        pv = _dot(p.reshape(R * tq, tk).astype(BF16), vs_ref[pl.ds(k0, tk), :]).reshape(R, tq, hd)
        return m_new, l, alpha * acc + pv

    def step(kt, carry):
        carry = attend(kt, carry, False)
        scores(kt + 1)
        return carry

    init = (jnp.full((R, tq, 1), NEG, F32), jnp.zeros((R, tq, 1), F32), jnp.zeros((R, tq, hd), F32))
    kdiag = t0 // tk
    scores(0)
    carry = lax.fori_loop(0, kdiag, step, init)
    _, l_s, acc_s = attend(kdiag, carry, True)
    o_sel = acc_s / l_s

    wlen = WIN + tq
    w0 = pl.multiple_of(jnp.maximum(t0 - WIN, 0), tq)
    kw = kw_ref[pl.ds(w0, wlen), :]
    vw = vw_ref[pl.ds(w0, wlen), :]
    kpos = w0 + lax.broadcasted_iota(jnp.int32, (1, wlen), 1)
    wbias = jnp.where((kpos <= rowpos) & (rowpos - kpos < WIN), 0.0, NEG)
    sw = _dot_nt(q4, kw).reshape(R, tq, wlen) + wbias[None]
    mw = jnp.max(sw, axis=-1, keepdims=True)
    pw = jnp.exp2(sw - mw)
    lw = jnp.sum(pw, axis=-1, keepdims=True)
    o_win = _dot(pw.reshape(R * tq, wlen).astype(BF16), vw).reshape(R, tq, hd) / lw

    ng = R * 3
    graw = jnp.where(g == 0, s_ref[:, S_NG:S_NG + ng], s_ref[:, S_NG + ng:S_NG + 2 * ng])
    gates = _sigmoid(graw)
    for r in range(R):
        oc = oc_ref[:, r * hd:(r + 1) * hd].astype(F32)
        out = (gates[:, 3 * r:3 * r + 1] * oc + gates[:, 3 * r + 1:3 * r + 2] * o_sel[r]
               + gates[:, 3 * r + 2:3 * r + 3] * o_win[r])
        o_ref[:, r * hd:(r + 1) * hd] = out.astype(BF16)


def _nsa_call(main, small, sel, o_cmp, B, T, tq=128, tk=512):
    n = main.shape[0]
    nq = T // tq
    nsel = T // SEL_BLOCK
    G = NSA_KV_HEADS
    hd = NSA_HD
    qw = NSA_GROUP * hd
    tk = min(tk, T)
    kvb = C_KV // hd

    def kvspec(slot):
        return pl.BlockSpec((T, hd), lambda b, g, t: (b, kvb + 2 * slot + g))

    return pl.pallas_call(
        functools.partial(_nsa_kernel, tq=tq, tk=tk, T=T, nsel=nsel),
        grid=(B, G, nq),
        in_specs=[pl.BlockSpec((tq, qw), lambda b, g, t: (b * nq + t, C_NQ // qw + g)),
                  kvspec(2), kvspec(3), kvspec(4), kvspec(5),
                  pl.BlockSpec((1, 1, tq, nsel), lambda b, g, t: (b, g, t, 0)),
                  pl.BlockSpec((tq, qw), lambda b, g, t: (b * nq + t, g)),
                  pl.BlockSpec((tq, LANES), lambda b, g, t: (b * nq + t, 0))],
        out_specs=pl.BlockSpec((tq, qw), lambda b, g, t: (b * nq + t, g)),
        out_shape=jax.ShapeDtypeStruct((n, NSA_HEADS * hd), BF16),
        scratch_shapes=[pltpu.VMEM((2, NSA_GROUP * tq, tk), F32)],
        compiler_params=_params("parallel", "parallel", "parallel"),
        name="nsa",
    )(main, main, main, main, main, sel, o_cmp, small)


def _merge_kernel(yg_ref, yn_ref, mg_ref, x_ref, g1_ref, sc_ref, sh_ref, n2_ref, pg_ref, pn_ref, wo_ref,
                  rw_ref, rb_ref, h_ref, u_ref, rt_ref, cnt_ref, carry_sc, *, tm):
    d = D_MODEL

    @pl.when(pl.program_id(0) == 0)
    def _():
        carry_sc[...] = jnp.zeros_like(carry_sc)

    a = _dot(yg_ref[...], pg_ref[...])
    b = _dot(yn_ref[...], pn_ref[...])
    mg = _sigmoid(mg_ref[...].astype(F32))
    merged = mg[:, :d] * a + mg[:, d:] * b
    mix = _dot(merged.astype(BF16), wo_ref[...])
    h = x_ref[...] + g1_ref[0] * mix
    h_ref[...] = h
    y = h * lax.rsqrt(jnp.mean(h * h, axis=-1, keepdims=True) + RMS_EPS) * n2_ref[...]
    u = y * (1.0 + sc_ref[0]) + sh_ref[0]
    u_ref[...] = u

    lane = lax.broadcasted_iota(jnp.int32, (1, LANES), 1).astype(F32)
    logits = jnp.where(lane < N_EXPERTS, _dot3(u, rw_ref[...]) + rb_ref[...], -jnp.inf)
    vals, idxs = [], []
    cur = logits
    for _ in range(TOP_K):
        m = jnp.max(cur, axis=-1, keepdims=True)
        idx = jnp.min(jnp.where(cur == m, lane, float(LANES)), axis=-1, keepdims=True)
        vals.append(m)
        idxs.append(idx)
        cur = jnp.where(lane == idx, -jnp.inf, cur)
    es = [jnp.exp(v - vals[0]) for v in vals]
    den = es[0] + es[1] + es[2] + es[3]
    onehots = [lane == idx for idx in idxs]
    oh = jnp.zeros((tm, LANES), F32)
    for o in onehots:
        oh = oh + jnp.where(o, 1.0, 0.0)
    row = lax.broadcasted_iota(jnp.int32, (tm, tm), 0)
    col = lax.broadcasted_iota(jnp.int32, (tm, tm), 1)
    tril = jnp.where(col <= row, 1.0, 0.0).astype(BF16)
    cnt = _dot(tril, oh.astype(BF16))
    tot = cnt + carry_sc[0:1, :]
    route = jnp.zeros((tm, LANES), F32)
    for k in range(TOP_K):
        pos = jnp.sum(jnp.where(onehots[k], tot - 1.0, 0.0), axis=-1, keepdims=True)
        route = jnp.where(lane == k, es[k] / den, route)
        route = jnp.where(lane == TOP_K + k, idxs[k], route)
        route = jnp.where(lane == 2 * TOP_K + k, pos, route)
    rt_ref[...] = route
    newc = tot[tm - 1:tm, :]
    carry_sc[...] = jnp.broadcast_to(newc, carry_sc.shape)
    cnt_ref[...] = jnp.broadcast_to(newc, cnt_ref.shape)


def _merge_call(y_gla, y_nsa, main, x2, gate1, scale2, shift2, n2g, pg, pn, wo, rw, rb, T, tm=512):
    n, d = x2.shape
    tpb = T // tm
    bmap = lambda i: (i // tpb, 0, 0)
    full = lambda i: (0, 0)
    return pl.pallas_call(
        functools.partial(_merge_kernel, tm=tm),
        grid=(n // tm,),
        in_specs=[pl.BlockSpec((tm, d), lambda i: (i, 0)),
                  pl.BlockSpec((tm, d), lambda i: (i, 0)),
                  pl.BlockSpec((tm, 2 * d), lambda i: (i, C_MG // (2 * d))),
                  pl.BlockSpec((tm, d), lambda i: (i, 0)),
                  pl.BlockSpec((1, 1, d), bmap), pl.BlockSpec((1, 1, d), bmap), pl.BlockSpec((1, 1, d), bmap),
                  pl.BlockSpec((1, d), full),
                  pl.BlockSpec((d, d), full), pl.BlockSpec((d, d), full), pl.BlockSpec((d, d), full),
                  pl.BlockSpec((d, LANES), full), pl.BlockSpec((1, LANES), full)],
        out_specs=[pl.BlockSpec((tm, d), lambda i: (i, 0)),
                   pl.BlockSpec((tm, d), lambda i: (i, 0)),
                   pl.BlockSpec((tm, LANES), lambda i: (i, 0)),
                   pl.BlockSpec((8, LANES), full)],
        out_shape=[jax.ShapeDtypeStruct((n, d), F32), jax.ShapeDtypeStruct((n, d), F32),
                   jax.ShapeDtypeStruct((n, LANES), F32), jax.ShapeDtypeStruct((8, LANES), F32)],
        scratch_shapes=[pltpu.VMEM((8, LANES), F32)],
        compiler_params=_params("arbitrary"),
        name="merge",
    )(y_gla, y_nsa, main, x2, gate1, scale2, shift2, n2g, pg, pn, wo, rw, rb)


SUBLANES = 8
ROW_TILES = D_MODEL // LANES
assert ROW_TILES == SUBLANES


def _to_token_tiles(dst_ref, src, rows):
    for j in range(ROW_TILES):
        dst_ref[pl.ds(j, rows, stride=SUBLANES), :] = src[:, j * LANES:(j + 1) * LANES]


def _from_token_tiles(src_ref, rows):
    return [src_ref[pl.ds(j, rows, stride=SUBLANES), :] for j in range(ROW_TILES)]


def _dispatch_kernel(dest_ref, pend_ref, u_ref, xs_ref, xt, zbuf, sem, zsem, *, td):
    base = pl.program_id(0) * (td * TOP_K)
    blk = MOE_ROWS * SUBLANES

    @pl.when(pl.program_id(0) == 0)
    def _():
        zbuf[...] = jnp.zeros_like(zbuf)

        def last_block(e):
            end = pend_ref[e]
            prev = pend_ref[e - 1] if e else 0
            start = pl.multiple_of(jnp.maximum(end - MOE_ROWS, 0) * SUBLANES, blk)
            return end > prev, pltpu.make_async_copy(zbuf, xs_ref.at[pl.ds(start, blk)], zsem)

        def unused_block(b):
            return b * MOE_ROWS >= pend_ref[N_EXPERTS - 1], pltpu.make_async_copy(
                zbuf, xs_ref.at[pl.ds(b * blk, blk)], zsem)

        nblk = xs_ref.shape[0] // blk
        fills = [functools.partial(last_block, e) for e in range(N_EXPERTS)]
        fills += [functools.partial(unused_block, b) for b in range(nblk - N_EXPERTS, nblk)]
        for f in fills:
            cond, fill = f()
            pl.when(cond)(fill.start)
        for f in fills:
            cond, fill = f()
            pl.when(cond)(fill.wait)

    _to_token_tiles(xt, u_ref[...], td)

    def row_copy(r, slot):
        return pltpu.make_async_copy(xt.at[pl.ds(pl.multiple_of(r * SUBLANES, SUBLANES), SUBLANES)],
                                     xs_ref.at[pl.ds(pl.multiple_of(slot * SUBLANES, SUBLANES), SUBLANES)], sem)

    def issue(r, c):
        for k in range(TOP_K):
            row_copy(r, dest_ref[base + r * TOP_K + k]).start(priority=k % 2)
        return c

    def drain(r, c):
        for k in range(TOP_K):
            row_copy(r, dest_ref[base + r * TOP_K + k]).wait()
        return c

    lax.fori_loop(0, td, issue, 0)
    lax.fori_loop(0, td, drain, 0)


def _dispatch_call(dest, pends, u2, p_rows, td=256):
    n, d = u2.shape
    return pl.pallas_call(
        functools.partial(_dispatch_kernel, td=td),
        grid_spec=pltpu.PrefetchScalarGridSpec(
            num_scalar_prefetch=2,
            grid=(n // td,),
            in_specs=[pl.BlockSpec((td, d), lambda i, dest, pends: (i, 0))],
            out_specs=pl.BlockSpec(memory_space=pl.ANY),
            scratch_shapes=[pltpu.VMEM((td * SUBLANES, LANES), F32), pltpu.VMEM((MOE_ROWS * SUBLANES, LANES), F32),
                            pltpu.SemaphoreType.DMA(()), pltpu.SemaphoreType.DMA(())]),
        out_shape=jax.ShapeDtypeStruct((p_rows * SUBLANES, LANES), F32),
        compiler_params=_params("arbitrary"),
        name="dispatch",
    )(dest, pends, u2)


def _moe_kernel(be_ref, na_ref, x_ref, w1_ref, b1_ref, w2_ref, b2_ref, y_ref, w1b, w2b):
    i = pl.program_id(0)
    e = be_ref[i]
    prev = be_ref[jnp.maximum(i - 1, 0)]
    active = i < na_ref[0]
    f = D_EXPERT
    m = MOE_ROWS

    @pl.when(active & ((i == 0) | (e != prev)))
    def _():
        w1b[...] = w1_ref[0].astype(BF16)
        w2b[...] = w2_ref[0].astype(BF16)

    @pl.when(active)
    def _():
        x = jnp.concatenate([t.astype(BF16) for t in _from_token_tiles(x_ref, m)], axis=1)
        hb = _dot(x, w1b[...]) + b1_ref[0]
        gate = jnp.minimum(hb[:, :f], SWIGLU_LIMIT)
        up = jnp.clip(hb[:, f:], -SWIGLU_LIMIT, SWIGLU_LIMIT)
        act = gate * _sigmoid(SWIGLU_ALPHA * gate) * (up + 1.0)
        _to_token_tiles(y_ref, _dot(act.astype(BF16), w2b[...]) + b2_ref[0], m)

    @pl.when(jnp.logical_not(active))
    def _():
        y_ref[...] = jnp.zeros_like(y_ref)


def _moe_call(blk_e, nact, xs, w1, b1, w2, b2):
    d, f2 = w1.shape[1], w1.shape[2]
    blk = MOE_ROWS * SUBLANES
    return pl.pallas_call(
        _moe_kernel,
        grid_spec=pltpu.PrefetchScalarGridSpec(
            num_scalar_prefetch=2,
            grid=(xs.shape[0] // blk,),
            in_specs=[pl.BlockSpec((blk, LANES), lambda i, be, na: (i, 0)),
                      pl.BlockSpec((1, d, f2), lambda i, be, na: (be[i], 0, 0)),
                      pl.BlockSpec((1, 1, f2), lambda i, be, na: (be[i], 0, 0)),
                      pl.BlockSpec((1, f2 // 2, d), lambda i, be, na: (be[i], 0, 0)),
                      pl.BlockSpec((1, 1, d), lambda i, be, na: (be[i], 0, 0))],
            out_specs=pl.BlockSpec((blk, LANES), lambda i, be, na: (i, 0)),
            scratch_shapes=[pltpu.VMEM((d, f2), BF16), pltpu.VMEM((f2 // 2, d), BF16)]),
        out_shape=jax.ShapeDtypeStruct(xs.shape, F32),
        compiler_params=_params("arbitrary"),
        name="moe",
    )(blk_e, nact, xs, w1, b1, w2, b2)


def _combine_kernel(dest_ref, h_ref, rt_ref, g2_ref, fg_ref, ys_ref, o_ref, buf, sems, *, tc):
    i = pl.program_id(0)
    slot = i % 2

    def row_copy(step, sl, r, k):
        src = dest_ref[step * (tc * TOP_K) + r * TOP_K + k]
        return pltpu.make_async_copy(
            ys_ref.at[pl.ds(pl.multiple_of(src * SUBLANES, SUBLANES), SUBLANES)],
            buf.at[sl, k, pl.ds(pl.multiple_of(r * SUBLANES, SUBLANES), SUBLANES)], sems.at[sl])

    def issue(step, sl):
        def body(r, c):
            for k in range(TOP_K):
                row_copy(step, sl, r, k).start(priority=k % 2)
            return c
        lax.fori_loop(0, tc, body, 0)

    def drain(step, sl):
        def body(r, c):
            for k in range(TOP_K):
                row_copy(step, sl, r, k).wait()
            return c
        lax.fori_loop(0, tc, body, 0)

    @pl.when(i == 0)
    def _():
        issue(0, 0)

    @pl.when(i + 1 < pl.num_programs(0))
    def _():
        issue(i + 1, 1 - slot)

    drain(i, slot)
    rt = rt_ref[...]
    slabs = [_from_token_tiles(buf.at[slot, k], tc) for k in range(TOP_K)]
    cols = []
    for j in range(ROW_TILES):
        col = rt[:, 0:1] * slabs[0][j]
        for k in range(1, TOP_K):
            col = col + rt[:, k:k + 1] * slabs[k][j]
        cols.append(col)
    moe = jnp.concatenate(cols, axis=1)
    hh = h_ref[...] + g2_ref[0] * moe
    o_ref[...] = hh * lax.rsqrt(jnp.mean(hh * hh, axis=-1, keepdims=True) + RMS_EPS) * fg_ref[...]


def _combine_call(dest, h, route, gate2, fg, ys, T, tc=128):
    n, d = h.shape
    tpb = T // tc
    return pl.pallas_call(
        functools.partial(_combine_kernel, tc=tc),
        grid_spec=pltpu.PrefetchScalarGridSpec(
            num_scalar_prefetch=1,
            grid=(n // tc,),
            in_specs=[pl.BlockSpec((tc, d), lambda i, dest: (i, 0)),
                      pl.BlockSpec((tc, LANES), lambda i, dest: (i, 0)),
                      pl.BlockSpec((1, 1, d), lambda i, dest: (i // tpb, 0, 0)),
                      pl.BlockSpec((1, d), lambda i, dest: (0, 0)),
                      pl.BlockSpec(memory_space=pl.ANY)],
            out_specs=pl.BlockSpec((tc, d), lambda i, dest: (i, 0)),
            scratch_shapes=[pltpu.VMEM((2, TOP_K, tc * SUBLANES, LANES), F32), pltpu.SemaphoreType.DMA((2,))]),
        out_shape=jax.ShapeDtypeStruct((n, d), F32),
        compiler_params=_params("arbitrary"),
        name="combine",
    )(dest, h, route, gate2, fg, ys)


def _mixer(x2, mod, norm1_g, w_in, gla_wa2, gla_ba, gla_norm_g, cmp_pe, cmp_w1, cmp_w2, B, T):
    d = D_MODEL
    G = NSA_KV_HEADS
    hd = NSA_HD
    shift1 = mod[:, 0 * d:1 * d].reshape(B, 1, d)
    scale1 = mod[:, 1 * d:2 * d].reshape(B, 1, d)
    w_main = jnp.concatenate([w_in[:, :_O_GA], w_in[:, _O_NQ:_O_KV], w_in[:, _O_MG:_O_END],
                              w_in[:, _O_KV:_O_NG]], axis=1).astype(BF16)
    w_small = jnp.concatenate([w_in[:, _O_GA:_O_NQ], w_in[:, _O_NG:_O_MG],
                               jnp.zeros((d, LANES - GLA_GATE_RANK - NSA_HEADS * 3), F32)], axis=1).astype(BF16)
    main, small, kv32 = _inproj_call(x2, norm1_g.reshape(1, d), scale1, shift1, w_main, w_small, T)
    y_gla = _gla_call(main, small, gla_wa2, gla_ba.reshape(1, -1), gla_norm_g.reshape(1, -1), B, T)
    kvc = _compress_call(kv32, cmp_pe.reshape(2, 1, CMP_LEN * hd), cmp_w1.astype(BF16), cmp_w2.astype(BF16), B, T)
    o_cmp, sel = _cmpsel_call(main, kvc, B, T)
    y_nsa = _nsa_call(main, small, sel, o_cmp, B, T)
    return main, y_gla, y_nsa


def _route_tables(route, counts, n):
    m = MOE_ROWS
    e = route[:, TOP_K:2 * TOP_K].astype(jnp.int32)
    pos = route[:, 2 * TOP_K:3 * TOP_K].astype(jnp.int32)
    cnt = counts[0, :N_EXPERTS].astype(jnp.int32)
    padded = ((cnt + m - 1) // m) * m
    pends = jnp.cumsum(padded)
    pstarts = pends - padded
    onehot = e[..., None] == jnp.arange(N_EXPERTS, dtype=jnp.int32)
    dest = (jnp.sum(jnp.where(onehot, pstarts, 0), axis=-1) + pos).reshape(-1)
    p_rows = n * TOP_K + N_EXPERTS * m
    nblk = p_rows // m
    starts = jnp.arange(nblk, dtype=jnp.int32) * m
    blk_e = jnp.minimum(jnp.sum((pends[None, :] <= starts[:, None]).astype(jnp.int32), axis=1), N_EXPERTS - 1)
    nact = (pends[-1:] // m).astype(jnp.int32)
    return dest, pends.astype(jnp.int32), blk_e, nact, p_rows


def kernel(x, c, w_ada, b_ada, norm1_g, w_in, gla_wa2, gla_ba, gla_norm_g, cmp_pe, cmp_w1, cmp_w2, proj_gla,
           proj_nsa, w_out, norm2_g, router_w, router_b, moe_w1, moe_b1, moe_w2, moe_b2, final_g):
    B, T, d = x.shape
    n = B * T
    assert w_ada.shape[0] == 1 and d == D_MODEL and B <= 8, "single-layer block of width D_MODEL"
    l = 0
    h = x.reshape(n, d)
    c8 = jnp.concatenate([c, jnp.zeros((8 - B, d), c.dtype)], axis=0)
    mod = _mod_call(c8, w_ada[l], b_ada[l].reshape(1, -1))[:B]
    gate1 = mod[:, 2 * d:3 * d].reshape(B, 1, d)
    shift2 = mod[:, 3 * d:4 * d].reshape(B, 1, d)
    scale2 = mod[:, 4 * d:5 * d].reshape(B, 1, d)
    gate2 = mod[:, 5 * d:6 * d].reshape(B, 1, d)
    main, y_gla, y_nsa = _mixer(h, mod, norm1_g[l], w_in[l], gla_wa2[l], gla_ba[l], gla_norm_g[l],
                                cmp_pe[l], cmp_w1[l], cmp_w2[l], B, T)
    rw = jnp.concatenate([router_w[l], jnp.zeros((d, LANES - N_EXPERTS), F32)], axis=1)
    rb = jnp.concatenate([router_b[l], jnp.zeros((LANES - N_EXPERTS,), F32)]).reshape(1, LANES)
    h1, u2, route, counts = _merge_call(
        y_gla, y_nsa, main, h, gate1, scale2, shift2, norm2_g[l].reshape(1, d),
        proj_gla[l].astype(BF16), proj_nsa[l].astype(BF16), w_out[l].astype(BF16), rw, rb, T)
    dest, pends, blk_e, nact, p_rows = _route_tables(route, counts, n)
    xs = _dispatch_call(dest, pends, u2, p_rows)
    ys = _moe_call(blk_e, nact, xs, moe_w1[l], moe_b1[l].reshape(N_EXPERTS, 1, -1), moe_w2[l],
                   moe_b2[l].reshape(N_EXPERTS, 1, -1))
    out = _combine_call(dest, h1, route, gate2, final_g.reshape(1, d), ys, T)
    return out.reshape(B, T, d)
```

```python
import functools

import jax
import jax.numpy as jnp
from jax import lax
from jax.experimental import pallas as pl
from jax.experimental.pallas import tpu as pltpu

F32 = jnp.float32
BF16 = jnp.bfloat16

D_MODEL = 1024
GLA_HEADS = 4
GLA_DK = 128
GLA_DV = 256
GLA_GATE_RANK = 16
GLA_TAU = 16.0
GLA_CHUNK = 64
NSA_HEADS = 8
NSA_KV_HEADS = 2
NSA_HD = 128
NSA_GROUP = NSA_HEADS // NSA_KV_HEADS
CMP_LEN = 32
CMP_STRIDE = 16
CMP_HIDDEN = 2 * NSA_HD
SEL_BLOCK = 64
SEL_TOPK = 16
WIN = 512
N_EXPERTS = 32
TOP_K = 4
D_EXPERT = D_MODEL
SWIGLU_LIMIT = 7.0
SWIGLU_ALPHA = 1.702
RMS_EPS = 1e-5

LANES = 128
NEG = -1e30
LOG2E = 1.4426950408889634
MOE_ROWS = 512
VMEM_LIMIT = 56 * 1024 * 1024

C_GQ, C_GK, C_GV, C_GR, C_NQ, C_MG, C_KV = 0, 512, 1024, 2048, 3072, 4096, 6144
MAIN_COLS = 7680
CMP_KV_COLS = 2 * NSA_KV_HEADS * NSA_HD
S_GA, S_NG = 0, 16

_O_GA = 3072
_O_NQ = 3088
_O_KV = 4112
_O_NG = 5648
_O_MG = 5672
_O_END = 7720


def _dot(a, b):
    return jnp.dot(a, b, preferred_element_type=F32)


def _dot_nt(a, b):
    return lax.dot_general(a, b, (((1,), (1,)), ((), ())), preferred_element_type=F32)


def _dot_tn(a, b):
    return lax.dot_general(a, b, (((0,), (0,)), ((), ())), preferred_element_type=F32)


def _split(a):
    hi = a.astype(BF16)
    lo = (a - hi.astype(F32)).astype(BF16)
    return hi, lo


def _dot3(a, b):
    ah, al = _split(a)
    bh, bl = _split(b)
    return _dot(ah, bh) + _dot(ah, bl) + _dot(al, bh)


def _sigmoid(x):
    return 1.0 / (1.0 + jnp.exp(-x))


def _params(*sem):
    return pltpu.CompilerParams(dimension_semantics=sem, vmem_limit_bytes=VMEM_LIMIT)


def _mod_kernel(c_ref, w_ref, b_ref, o_ref):
    c = c_ref[...]
    cond = c * _sigmoid(c)
    o_ref[...] = _dot3(cond, w_ref[...]) + b_ref[...]


def _mod_call(c8, w_ada, b_ada):
    d = c8.shape[1]
    n = w_ada.shape[1]
    return pl.pallas_call(
        _mod_kernel,
        grid=(n // d,),
        in_specs=[pl.BlockSpec((8, d), lambda j: (0, 0)),
                  pl.BlockSpec((d, d), lambda j: (0, j)),
                  pl.BlockSpec((1, d), lambda j: (0, j))],
        out_specs=pl.BlockSpec((8, d), lambda j: (0, j)),
        out_shape=jax.ShapeDtypeStruct((8, n), F32),
        compiler_params=_params("parallel"),
        name="mod",
    )(c8, w_ada, b_ada)


def _inproj_kernel(x_ref, g_ref, sc_ref, sh_ref, w_ref, ws_ref, o_ref, s_ref, kv_ref, u_sc, *, kv_off):
    @pl.when(pl.program_id(1) == 0)
    def _():
        xf = x_ref[...]
        ms = jnp.mean(xf * xf, axis=-1, keepdims=True)
        y = xf * lax.rsqrt(ms + RMS_EPS) * g_ref[...]
        u = (y * (1.0 + sc_ref[0]) + sh_ref[0]).astype(BF16)
        u_sc[...] = u
        s_ref[...] = _dot(u, ws_ref[...])

    acc = _dot(u_sc[...], w_ref[...])
    o_ref[...] = acc.astype(BF16)

    @pl.when(pl.program_id(1) == pl.num_programs(1) - 1)
    def _():
        kv_ref[...] = acc[:, kv_off:kv_off + CMP_KV_COLS]


def _inproj_call(x2, g, scale, shift, w_main, w_small, T, tm=1024, tn=1920):
    n, d = x2.shape
    tpb = T // tm
    ncol = MAIN_COLS // tn
    kv_off = C_KV - (ncol - 1) * tn
    assert kv_off >= 0 and kv_off % LANES == 0 and kv_off + CMP_KV_COLS <= tn
    return pl.pallas_call(
        functools.partial(_inproj_kernel, kv_off=kv_off),
        grid=(n // tm, ncol),
        in_specs=[pl.BlockSpec((tm, d), lambda i, j: (i, 0)),
                  pl.BlockSpec((1, d), lambda i, j: (0, 0)),
                  pl.BlockSpec((1, 1, d), lambda i, j: (i // tpb, 0, 0)),
                  pl.BlockSpec((1, 1, d), lambda i, j: (i // tpb, 0, 0)),
                  pl.BlockSpec((d, tn), lambda i, j: (0, j)),
                  pl.BlockSpec((d, LANES), lambda i, j: (0, 0))],
        out_specs=[pl.BlockSpec((tm, tn), lambda i, j: (i, j)),
                   pl.BlockSpec((tm, LANES), lambda i, j: (i, 0)),
                   pl.BlockSpec((tm, CMP_KV_COLS), lambda i, j: (i, 0))],
        out_shape=[jax.ShapeDtypeStruct((n, MAIN_COLS), BF16),
                   jax.ShapeDtypeStruct((n, LANES), F32),
                   jax.ShapeDtypeStruct((n, CMP_KV_COLS), F32)],
        scratch_shapes=[pltpu.VMEM((tm, d), BF16)],
        compiler_params=_params("parallel", "arbitrary"),
        name="inproj",
    )(x2, g, scale, shift, w_main, w_small)


def _gla_kernel(q_ref, k_ref, v_ref, r_ref, s_ref, wa_ref, ba_ref, g_ref, o_ref, st_sc, *, tt):
    @pl.when(pl.program_id(1) == 0)
    def _():
        st_sc[...] = jnp.zeros_like(st_sc)

    C = GLA_CHUNK
    nc = tt // C
    dk, dv = GLA_DK, GLA_DV
    ga = s_ref[:, S_GA:S_GA + GLA_GATE_RANK]
    z = _dot3(ga, wa_ref[...]) + ba_ref[...]
    la = (jnp.minimum(z, 0.0) - jnp.log(1.0 + jnp.exp(-jnp.abs(z)))) * (1.0 / GLA_TAU)
    row = lax.broadcasted_iota(jnp.int32, (tt, tt), 0)
    col = lax.broadcasted_iota(jnp.int32, (tt, tt), 1)
    causal = ((row // C) == (col // C)) & (col <= row)
    tril = jnp.where(causal, 1.0, 0.0).astype(BF16)
    hi, lo = _split(la)
    bc_all = _dot(tril, hi) + _dot(tril, lo)
    for h in range(GLA_HEADS):
        bc = bc_all[:, h * dk:(h + 1) * dk]
        bls = [bc[(c + 1) * C - 1:(c + 1) * C, :] for c in range(nc)]
        blr = jnp.concatenate([jnp.broadcast_to(b, (C, dk)) for b in bls], axis=0)
        q = q_ref[:, h * dk:(h + 1) * dk].astype(F32) * (dk ** -0.5)
        k = k_ref[:, h * dk:(h + 1) * dk].astype(F32)
        v = v_ref[:, h * dv:(h + 1) * dv]
        qe = (q * jnp.exp(bc)).astype(BF16)
        ke = (k * jnp.exp(-bc)).astype(BF16)
        kd = (k * jnp.exp(blr - bc)).astype(BF16)
        a = jnp.where(causal, _dot_nt(qe, ke), 0.0)
        intra = _dot(a.astype(BF16), v)
        st = st_sc[h]
        inter = []
        for c in range(nc):
            sl = slice(c * C, (c + 1) * C)
            inter.append(_dot_nt(qe[sl], st.astype(BF16)))
            st = st * jnp.exp(bls[c]) + _dot_tn(v[sl], kd[sl])
        st_sc[h] = st
        o = intra + jnp.concatenate(inter, axis=0)
        y = o * lax.rsqrt(jnp.mean(o * o, axis=-1, keepdims=True) + RMS_EPS) * g_ref[...]
        r = r_ref[:, h * dv:(h + 1) * dv].astype(F32)
        o_ref[:, h * dv:(h + 1) * dv] = (y * (r * _sigmoid(r))).astype(BF16)


def _gla_call(main, small, wa2, ba, norm_g, B, T, tt=256):
    n = main.shape[0]
    nt = T // tt
    H = GLA_HEADS
    rowmap = lambda b, t: b * nt + t
    return pl.pallas_call(
        functools.partial(_gla_kernel, tt=tt),
        grid=(B, nt),
        in_specs=[pl.BlockSpec((tt, H * GLA_DK), lambda b, t: (rowmap(b, t), C_GQ // (H * GLA_DK))),
                  pl.BlockSpec((tt, H * GLA_DK), lambda b, t: (rowmap(b, t), C_GK // (H * GLA_DK))),
                  pl.BlockSpec((tt, H * GLA_DV), lambda b, t: (rowmap(b, t), C_GV // (H * GLA_DV))),
                  pl.BlockSpec((tt, H * GLA_DV), lambda b, t: (rowmap(b, t), C_GR // (H * GLA_DV))),
                  pl.BlockSpec((tt, LANES), lambda b, t: (rowmap(b, t), 0)),
                  pl.BlockSpec((GLA_GATE_RANK, H * GLA_DK), lambda b, t: (0, 0)),
                  pl.BlockSpec((1, H * GLA_DK), lambda b, t: (0, 0)),
                  pl.BlockSpec((1, GLA_DV), lambda b, t: (0, 0))],
        out_specs=pl.BlockSpec((tt, H * GLA_DV), lambda b, t: (rowmap(b, t), 0)),
        out_shape=jax.ShapeDtypeStruct((n, H * GLA_DV), BF16),
        scratch_shapes=[pltpu.VMEM((H, GLA_DV, GLA_DK), F32)],
        compiler_params=_params("parallel", "arbitrary"),
        name="gla",
    )(main, main, main, main, small, wa2, ba, norm_g)


def _compress_kernel(x_ref, pe_ref, w1_ref, w2_ref, o_ref):
    half = CMP_STRIDE * NSA_HD
    nb = x_ref.shape[0] // CMP_STRIDE
    x = jnp.concatenate([x_ref[pl.ds(t, nb, stride=CMP_STRIDE), :].astype(BF16) for t in range(CMP_STRIDE)], axis=1)
    ylo = _dot(x, w1_ref[0, :half, :])
    yhi = _dot(x, w1_ref[0, half:, :])
    pe = jnp.broadcast_to(pe_ref[0], (8, 2 * half)).astype(BF16)
    pterm = _dot(pe, w1_ref[0])[0:1, :]
    pre = ylo + pltpu.roll(yhi, nb - 1, 0) + pterm
    hcu = pre * pre * pre
    hid = 0.5 * pre * (1.0 + jnp.tanh(0.7978845608028654 * (pre + 0.044715 * hcu)))
    o_ref[0, 0] = _dot(hid.astype(BF16), w2_ref[0]).astype(BF16)


def _compress_call(kv32, pe, w1, w2, B, T):
    G = NSA_KV_HEADS
    bg = B * G
    nb = T // CMP_STRIDE
    return pl.pallas_call(
        _compress_kernel,
        grid=(2, bg),
        in_specs=[pl.BlockSpec((T, NSA_HD), lambda s, i: (i // G, s * G + i % G)),
                  pl.BlockSpec((1, 1, CMP_LEN * NSA_HD), lambda s, i: (s, 0, 0)),
                  pl.BlockSpec((1, CMP_LEN * NSA_HD, CMP_HIDDEN), lambda s, i: (s, 0, 0)),
                  pl.BlockSpec((1, CMP_HIDDEN, NSA_HD), lambda s, i: (s, 0, 0))],
        out_specs=pl.BlockSpec((1, 1, nb, NSA_HD), lambda s, i: (s, i, 0, 0)),
        out_shape=jax.ShapeDtypeStruct((2, bg, nb, NSA_HD), BF16),
        compiler_params=_params("parallel", "parallel"),
        name="compress",
    )(kv32, pe, w1, w2)


def _cmpsel_kernel(q_ref, kc_ref, vc_ref, o_ref, sel_ref, *, tq, nsel):
    t0 = pl.program_id(2) * tq
    kc = kc_ref[0, 0]
    vc = vc_ref[0, 0]
    ncp = kc.shape[0]
    tpos = t0 + lax.broadcasted_iota(jnp.int32, (tq, 1), 0)
    cidx = lax.broadcasted_iota(jnp.int32, (1, ncp), 1)
    cmask = (cidx * CMP_STRIDE + (CMP_LEN - 1)) <= tpos
    psum = jnp.zeros((tq, ncp), F32)
    for r in range(NSA_GROUP):
        q = q_ref[:, r * NSA_HD:(r + 1) * NSA_HD]
        s = jnp.where(cmask, _dot_nt(q, kc) * (NSA_HD ** -0.5), NEG)
        m = jnp.max(s, axis=-1, keepdims=True)
        e = jnp.where(cmask, jnp.exp(s - m), 0.0)
        p = e / jnp.maximum(jnp.sum(e, axis=-1, keepdims=True), 1e-30)
        o_ref[:, r * NSA_HD:(r + 1) * NSA_HD] = _dot(p.astype(BF16), vc).astype(BF16)
        psum = psum + p
    ci = lax.broadcasted_iota(jnp.int32, (nsel, ncp), 1) * CMP_STRIDE
    si = lax.broadcasted_iota(jnp.int32, (nsel, ncp), 0) * SEL_BLOCK
    ov = jnp.where((ci < si + SEL_BLOCK) & (ci + CMP_LEN > si), 1.0, 0.0).astype(BF16)
    hi, lo = _split(psum)
    imp = _dot_nt(ov, hi) + _dot_nt(ov, lo)
    tlane = t0 + lax.broadcasted_iota(jnp.int32, (1, tq), 1)
    jblk = lax.broadcasted_iota(jnp.int32, (nsel, 1), 0)
    tblk = tlane // SEL_BLOCK
    forced = (jblk == 0) | (jblk == tblk) | (jblk == tblk - 1)
    valid = (jblk * SEL_BLOCK) <= tlane
    score = jnp.where(forced, jnp.inf, jnp.where(valid, imp, -jnp.inf))
    ngrp = nsel // 8
    groups = [score[8 * a:8 * a + 8, :] for a in range(ngrp)]
    sub = lax.broadcasted_iota(jnp.int32, (8, 1), 0)
    ranks = [jnp.zeros((8, tq), F32) for _ in range(ngrp)]
    for i in range(nsel):
        si_ = score[i:i + 1, :]
        for a in range(ngrp):
            if a < i // 8:
                beats = si_ > groups[a]
            elif a > i // 8:
                beats = si_ >= groups[a]
            else:
                beats = (si_ > groups[a]) | ((si_ == groups[a]) & (sub > i % 8))
            ranks[a] = ranks[a] + jnp.where(beats, 1.0, 0.0)
    rank = jnp.concatenate(ranks, axis=0)
    ntop = min(SEL_TOPK, nsel)
    chosen = jnp.where(valid & (rank < ntop), 1.0, 0.0)
    sel_ref[0, 0] = jnp.transpose(chosen).astype(BF16)


def _cmpsel_call(main, kvc, B, T, tq=512):
    n = main.shape[0]
    nq = T // tq
    nsel = T // SEL_BLOCK
    ncp = kvc.shape[2]
    G = NSA_KV_HEADS
    qw = NSA_GROUP * NSA_HD
    return pl.pallas_call(
        functools.partial(_cmpsel_kernel, tq=tq, nsel=nsel),
        grid=(B, G, nq),
        in_specs=[pl.BlockSpec((tq, qw), lambda b, g, t: (b * nq + t, C_NQ // qw + g)),
                  pl.BlockSpec((1, 1, ncp, NSA_HD), lambda b, g, t: (0, b * G + g, 0, 0)),
                  pl.BlockSpec((1, 1, ncp, NSA_HD), lambda b, g, t: (1, b * G + g, 0, 0))],
        out_specs=[pl.BlockSpec((tq, qw), lambda b, g, t: (b * nq + t, g)),
                   pl.BlockSpec((1, 1, tq, nsel), lambda b, g, t: (b, g, t, 0))],
        out_shape=[jax.ShapeDtypeStruct((n, NSA_HEADS * NSA_HD), BF16),
                   jax.ShapeDtypeStruct((B, G, T, nsel), BF16)],
        compiler_params=_params("parallel", "parallel", "parallel"),
        name="cmpsel",
    )(main, kvc, kvc)


def _nsa_kernel(q_ref, ks_ref, vs_ref, kw_ref, vw_ref, sel_ref, oc_ref, s_ref, o_ref, s_sc, *, tq, tk, T, nsel):
    R = NSA_GROUP
    hd = NSA_HD
    g = pl.program_id(1)
    t0 = pl.program_id(2) * tq
    qscale = (hd ** -0.5) * LOG2E
    q4 = jnp.concatenate([q_ref[:, r * hd:(r + 1) * hd] for r in range(R)], axis=0)
    q4 = (q4.astype(F32) * qscale).astype(BF16)
    rowpos = t0 + lax.broadcasted_iota(jnp.int32, (tq, 1), 0)
    selm = sel_ref[0, 0].astype(F32) - 1.0
    selm4 = jnp.concatenate([selm] * R, axis=0).astype(BF16)

    qa = jnp.concatenate([q4, selm4], axis=1)
    jcol = lax.broadcasted_iota(jnp.int32, (tk, nsel), 1)
    crow = lax.broadcasted_iota(jnp.int32, (tk, nsel), 0) // SEL_BLOCK
    kcol = lax.broadcasted_iota(jnp.int32, (1, tk), 1)

    def scores(kt):
        k0 = pl.multiple_of(kt * tk, tk)
        ind = jnp.where(jcol == crow + kt * (tk // SEL_BLOCK), 1e30, 0.0).astype(BF16)
        s_sc[kt % 2] = _dot_nt(qa, jnp.concatenate([ks_ref[pl.ds(k0, tk), :], ind], axis=1))

    def attend(kt, carry, diagonal):
        m, l, acc = carry
        k0 = pl.multiple_of(kt * tk, tk)
        s = s_sc[kt % 2].reshape(R, tq, tk)
        if diagonal:
            s = s + jnp.where((k0 + kcol) <= rowpos, 0.0, NEG)[None]
        m_new = jnp.maximum(m, jnp.max(s, axis=-1, keepdims=True))
        alpha = jnp.exp2(m - m_new)
        p = jnp.exp2(s - m_new)
        l = alpha * l + jnp.sum(p, axis=-1, keepdims=True)
        pv = _dot(p.reshape(R * tq, tk).astype(BF16), vs_ref[pl.ds(k0, tk), :]).reshape(R, tq, hd)
        return m_new, l, alpha * acc + pv

    def step(kt, carry):
        carry = attend(kt, carry, False)
        scores(kt + 1)
        return carry

    init = (jnp.full((R, tq, 1), NEG, F32), jnp.zeros((R, tq, 1), F32), jnp.zeros((R, tq, hd), F32))
    kdiag = t0 // tk
    scores(0)
    carry = lax.fori_loop(0, kdiag, step, init)
    _, l_s, acc_s = attend(kdiag, carry, True)
    o_sel = acc_s / l_s

    wlen = WIN + tq
    w0 = pl.multiple_of(jnp.maximum(t0 - WIN, 0), tq)
    kw = kw_ref[pl.ds(w0, wlen), :]
    vw = vw_ref[pl.ds(w0, wlen), :]
    kpos = w0 + lax.broadcasted_iota(jnp.int32, (1, wlen), 1)
    wbias = jnp.where((kpos <= rowpos) & (rowpos - kpos < WIN), 0.0, NEG)
    sw = _dot_nt(q4, kw).reshape(R, tq, wlen) + wbias[None]
    mw = jnp.max(sw, axis=-1, keepdims=True)
    pw = jnp.exp2(sw - mw)
    lw = jnp.sum(pw, axis=-1, keepdims=True)
    o_win = _dot(pw.reshape(R * tq, wlen).astype(BF16), vw).reshape(R, tq, hd) / lw

    ng = R * 3
    graw = jnp.where(g == 0, s_ref[:, S_NG:S_NG + ng], s_ref[:, S_NG + ng:S_NG + 2 * ng])
    gates = _sigmoid(graw)
    for r in range(R):
        oc = oc_ref[:, r * hd:(r + 1) * hd].astype(F32)
        out = (gates[:, 3 * r:3 * r + 1] * oc + gates[:, 3 * r + 1:3 * r + 2] * o_sel[r]
               + gates[:, 3 * r + 2:3 * r + 3] * o_win[r])
        o_ref[:, r * hd:(r + 1) * hd] = out.astype(BF16)


def _nsa_call(main, small, sel, o_cmp, B, T, tq=128, tk=1024):
    n = main.shape[0]
    nq = T // tq
    nsel = T // SEL_BLOCK
    G = NSA_KV_HEADS
    hd = NSA_HD
    qw = NSA_GROUP * hd
    tk = min(tk, T)
    kvb = C_KV // hd

    def kvspec(slot):
        return pl.BlockSpec((T, hd), lambda b, g, t: (b, kvb + 2 * slot + g))

    return pl.pallas_call(
        functools.partial(_nsa_kernel, tq=tq, tk=tk, T=T, nsel=nsel),
        grid=(B, G, nq),
        in_specs=[pl.BlockSpec((tq, qw), lambda b, g, t: (b * nq + t, C_NQ // qw + g)),
                  kvspec(2), kvspec(3), kvspec(4), kvspec(5),
                  pl.BlockSpec((1, 1, tq, nsel), lambda b, g, t: (b, g, t, 0)),
                  pl.BlockSpec((tq, qw), lambda b, g, t: (b * nq + t, g)),
                  pl.BlockSpec((tq, LANES), lambda b, g, t: (b * nq + t, 0))],
        out_specs=pl.BlockSpec((tq, qw), lambda b, g, t: (b * nq + t, g)),
        out_shape=jax.ShapeDtypeStruct((n, NSA_HEADS * hd), BF16),
        scratch_shapes=[pltpu.VMEM((2, NSA_GROUP * tq, tk), F32)],
        compiler_params=_params("parallel", "parallel", "parallel"),
        name="nsa",
    )(main, main, main, main, main, sel, o_cmp, small)


def _merge_kernel(yg_ref, yn_ref, mg_ref, x_ref, g1_ref, sc_ref, sh_ref, n2_ref, pg_ref, pn_ref, wo_ref,
                  rw_ref, rb_ref, h_ref, u_ref, rt_ref, cnt_ref, carry_sc, *, tm):
    d = D_MODEL

    @pl.when(pl.program_id(0) == 0)
    def _():
        carry_sc[...] = jnp.zeros_like(carry_sc)

    a = _dot(yg_ref[...], pg_ref[...])
    b = _dot(yn_ref[...], pn_ref[...])
    mg = _sigmoid(mg_ref[...].astype(F32))
    merged = mg[:, :d] * a + mg[:, d:] * b
    mix = _dot(merged.astype(BF16), wo_ref[...])
    h = x_ref[...] + g1_ref[0] * mix
    h_ref[...] = h
    y = h * lax.rsqrt(jnp.mean(h * h, axis=-1, keepdims=True) + RMS_EPS) * n2_ref[...]
    u = y * (1.0 + sc_ref[0]) + sh_ref[0]
    u_ref[...] = u

    lane = lax.broadcasted_iota(jnp.int32, (1, LANES), 1).astype(F32)
    logits = jnp.where(lane < N_EXPERTS, _dot3(u, rw_ref[...]) + rb_ref[...], -jnp.inf)
    vals, idxs = [], []
    cur = logits
    for _ in range(TOP_K):
        m = jnp.max(cur, axis=-1, keepdims=True)
        idx = jnp.min(jnp.where(cur == m, lane, float(LANES)), axis=-1, keepdims=True)
        vals.append(m)
        idxs.append(idx)
        cur = jnp.where(lane == idx, -jnp.inf, cur)
    es = [jnp.exp(v - vals[0]) for v in vals]
    den = es[0] + es[1] + es[2] + es[3]
    onehots = [lane == idx for idx in idxs]
    oh = jnp.zeros((tm, LANES), F32)
    for o in onehots:
        oh = oh + jnp.where(o, 1.0, 0.0)
    row = lax.broadcasted_iota(jnp.int32, (tm, tm), 0)
    col = lax.broadcasted_iota(jnp.int32, (tm, tm), 1)
    tril = jnp.where(col <= row, 1.0, 0.0).astype(BF16)
    cnt = _dot(tril, oh.astype(BF16))
    tot = cnt + carry_sc[0:1, :]
    route = jnp.zeros((tm, LANES), F32)
    for k in range(TOP_K):
        pos = jnp.sum(jnp.where(onehots[k], tot - 1.0, 0.0), axis=-1, keepdims=True)
        route = jnp.where(lane == k, es[k] / den, route)
        route = jnp.where(lane == TOP_K + k, idxs[k], route)
        route = jnp.where(lane == 2 * TOP_K + k, pos, route)
    rt_ref[...] = route
    newc = tot[tm - 1:tm, :]
    carry_sc[...] = jnp.broadcast_to(newc, carry_sc.shape)
    cnt_ref[...] = jnp.broadcast_to(newc, cnt_ref.shape)


def _merge_call(y_gla, y_nsa, main, x2, gate1, scale2, shift2, n2g, pg, pn, wo, rw, rb, T, tm=512):
    n, d = x2.shape
    tpb = T // tm
    bmap = lambda i: (i // tpb, 0, 0)
    full = lambda i: (0, 0)
    return pl.pallas_call(
        functools.partial(_merge_kernel, tm=tm),
        grid=(n // tm,),
        in_specs=[pl.BlockSpec((tm, d), lambda i: (i, 0)),
                  pl.BlockSpec((tm, d), lambda i: (i, 0)),
                  pl.BlockSpec((tm, 2 * d), lambda i: (i, C_MG // (2 * d))),
                  pl.BlockSpec((tm, d), lambda i: (i, 0)),
                  pl.BlockSpec((1, 1, d), bmap), pl.BlockSpec((1, 1, d), bmap), pl.BlockSpec((1, 1, d), bmap),
                  pl.BlockSpec((1, d), full),
                  pl.BlockSpec((d, d), full), pl.BlockSpec((d, d), full), pl.BlockSpec((d, d), full),
                  pl.BlockSpec((d, LANES), full), pl.BlockSpec((1, LANES), full)],
        out_specs=[pl.BlockSpec((tm, d), lambda i: (i, 0)),
                   pl.BlockSpec((tm, d), lambda i: (i, 0)),
                   pl.BlockSpec((tm, LANES), lambda i: (i, 0)),
                   pl.BlockSpec((8, LANES), full)],
        out_shape=[jax.ShapeDtypeStruct((n, d), F32), jax.ShapeDtypeStruct((n, d), F32),
                   jax.ShapeDtypeStruct((n, LANES), F32), jax.ShapeDtypeStruct((8, LANES), F32)],
        scratch_shapes=[pltpu.VMEM((8, LANES), F32)],
        compiler_params=_params("arbitrary"),
        name="merge",
    )(y_gla, y_nsa, main, x2, gate1, scale2, shift2, n2g, pg, pn, wo, rw, rb)


SUBLANES = 8
ROW_TILES = D_MODEL // LANES
assert ROW_TILES == SUBLANES


def _to_token_tiles(dst_ref, src, rows):
    for j in range(ROW_TILES):
        dst_ref[pl.ds(j, rows, stride=SUBLANES), :] = src[:, j * LANES:(j + 1) * LANES]


def _from_token_tiles(src_ref, rows):
    return [src_ref[pl.ds(j, rows, stride=SUBLANES), :] for j in range(ROW_TILES)]


def _dispatch_kernel(dest_ref, pend_ref, u_ref, xs_ref, xt, zbuf, sem, zsem, *, td):
    base = pl.program_id(0) * (td * TOP_K)
    blk = MOE_ROWS * SUBLANES

    @pl.when(pl.program_id(0) == 0)
    def _():
        zbuf[...] = jnp.zeros_like(zbuf)

        def last_block(e):
            end = pend_ref[e]
            prev = pend_ref[e - 1] if e else 0
            start = pl.multiple_of(jnp.maximum(end - MOE_ROWS, 0) * SUBLANES, blk)
            return end > prev, pltpu.make_async_copy(zbuf, xs_ref.at[pl.ds(start, blk)], zsem)

        def unused_block(b):
            return b * MOE_ROWS >= pend_ref[N_EXPERTS - 1], pltpu.make_async_copy(
                zbuf, xs_ref.at[pl.ds(b * blk, blk)], zsem)

        nblk = xs_ref.shape[0] // blk
        fills = [functools.partial(last_block, e) for e in range(N_EXPERTS)]
        fills += [functools.partial(unused_block, b) for b in range(nblk - N_EXPERTS, nblk)]
        for f in fills:
            cond, fill = f()
            pl.when(cond)(fill.start)
        for f in fills:
            cond, fill = f()
            pl.when(cond)(fill.wait)

    _to_token_tiles(xt, u_ref[...], td)

    def row_copy(r, slot):
        return pltpu.make_async_copy(xt.at[pl.ds(pl.multiple_of(r * SUBLANES, SUBLANES), SUBLANES)],
                                     xs_ref.at[pl.ds(pl.multiple_of(slot * SUBLANES, SUBLANES), SUBLANES)], sem)

    def issue(r, c):
        for k in range(TOP_K):
            row_copy(r, dest_ref[base + r * TOP_K + k]).start(priority=k % 2)
        return c

    def drain(r, c):
        for k in range(TOP_K):
            row_copy(r, dest_ref[base + r * TOP_K + k]).wait()
        return c

    lax.fori_loop(0, td, issue, 0)
    lax.fori_loop(0, td, drain, 0)


def _dispatch_call(dest, pends, u2, p_rows, td=512):
    n, d = u2.shape
    return pl.pallas_call(
        functools.partial(_dispatch_kernel, td=td),
        grid_spec=pltpu.PrefetchScalarGridSpec(
            num_scalar_prefetch=2,
            grid=(n // td,),
            in_specs=[pl.BlockSpec((td, d), lambda i, dest, pends: (i, 0))],
            out_specs=pl.BlockSpec(memory_space=pl.ANY),
            scratch_shapes=[pltpu.VMEM((td * SUBLANES, LANES), F32), pltpu.VMEM((MOE_ROWS * SUBLANES, LANES), F32),
                            pltpu.SemaphoreType.DMA(()), pltpu.SemaphoreType.DMA(())]),
        out_shape=jax.ShapeDtypeStruct((p_rows * SUBLANES, LANES), F32),
        compiler_params=_params("arbitrary"),
        name="dispatch",
    )(dest, pends, u2)


def _moe_kernel(be_ref, na_ref, x_ref, w1_ref, b1_ref, w2_ref, b2_ref, y_ref, w1b, w2b):
    i = pl.program_id(0)
    e = be_ref[i]
    prev = be_ref[jnp.maximum(i - 1, 0)]
    active = i < na_ref[0]
    f = D_EXPERT
    m = MOE_ROWS

    @pl.when(active & ((i == 0) | (e != prev)))
    def _():
        w1b[...] = w1_ref[0].astype(BF16)
        w2b[...] = w2_ref[0].astype(BF16)

    @pl.when(active)
    def _():
        x = jnp.concatenate([t.astype(BF16) for t in _from_token_tiles(x_ref, m)], axis=1)
        hb = _dot(x, w1b[...]) + b1_ref[0]
        gate = jnp.minimum(hb[:, :f], SWIGLU_LIMIT)
        up = jnp.clip(hb[:, f:], -SWIGLU_LIMIT, SWIGLU_LIMIT)
        act = gate * _sigmoid(SWIGLU_ALPHA * gate) * (up + 1.0)
        _to_token_tiles(y_ref, _dot(act.astype(BF16), w2b[...]) + b2_ref[0], m)

    @pl.when(jnp.logical_not(active))
    def _():
        y_ref[...] = jnp.zeros_like(y_ref)


def _moe_call(blk_e, nact, xs, w1, b1, w2, b2):
    d, f2 = w1.shape[1], w1.shape[2]
    blk = MOE_ROWS * SUBLANES
    return pl.pallas_call(
        _moe_kernel,
        grid_spec=pltpu.PrefetchScalarGridSpec(
            num_scalar_prefetch=2,
            grid=(xs.shape[0] // blk,),
            in_specs=[pl.BlockSpec((blk, LANES), lambda i, be, na: (i, 0)),
                      pl.BlockSpec((1, d, f2), lambda i, be, na: (be[i], 0, 0)),
                      pl.BlockSpec((1, 1, f2), lambda i, be, na: (be[i], 0, 0)),
                      pl.BlockSpec((1, f2 // 2, d), lambda i, be, na: (be[i], 0, 0)),
                      pl.BlockSpec((1, 1, d), lambda i, be, na: (be[i], 0, 0))],
            out_specs=pl.BlockSpec((blk, LANES), lambda i, be, na: (i, 0)),
            scratch_shapes=[pltpu.VMEM((d, f2), BF16), pltpu.VMEM((f2 // 2, d), BF16)]),
        out_shape=jax.ShapeDtypeStruct(xs.shape, F32),
        compiler_params=_params("arbitrary"),
        name="moe",
    )(blk_e, nact, xs, w1, b1, w2, b2)


def _combine_kernel(dest_ref, h_ref, rt_ref, g2_ref, fg_ref, ys_ref, o_ref, buf, sems, *, tc):
    i = pl.program_id(0)
    slot = i % 2

    def row_copy(step, sl, r, k):
        src = dest_ref[step * (tc * TOP_K) + r * TOP_K + k]
        return pltpu.make_async_copy(
            ys_ref.at[pl.ds(pl.multiple_of(src * SUBLANES, SUBLANES), SUBLANES)],
            buf.at[sl, k, pl.ds(pl.multiple_of(r * SUBLANES, SUBLANES), SUBLANES)], sems.at[sl])

    def issue(step, sl):
        def body(r, c):
            for k in range(TOP_K):
                row_copy(step, sl, r, k).start(priority=k % 2)
            return c
        lax.fori_loop(0, tc, body, 0)

    def drain(step, sl):
        def body(r, c):
            for k in range(TOP_K):
                row_copy(step, sl, r, k).wait()
            return c
        lax.fori_loop(0, tc, body, 0)

    @pl.when(i == 0)
    def _():
        issue(0, 0)

    @pl.when(i + 1 < pl.num_programs(0))
    def _():
        issue(i + 1, 1 - slot)

    drain(i, slot)
    rt = rt_ref[...]
    slabs = [_from_token_tiles(buf.at[slot, k], tc) for k in range(TOP_K)]
    cols = []
    for j in range(ROW_TILES):
        col = rt[:, 0:1] * slabs[0][j]
        for k in range(1, TOP_K):
            col = col + rt[:, k:k + 1] * slabs[k][j]
        cols.append(col)
    moe = jnp.concatenate(cols, axis=1)
    hh = h_ref[...] + g2_ref[0] * moe
    o_ref[...] = hh * lax.rsqrt(jnp.mean(hh * hh, axis=-1, keepdims=True) + RMS_EPS) * fg_ref[...]


def _combine_call(dest, h, route, gate2, fg, ys, T, tc=256):
    n, d = h.shape
    tpb = T // tc
    return pl.pallas_call(
        functools.partial(_combine_kernel, tc=tc),
        grid_spec=pltpu.PrefetchScalarGridSpec(
            num_scalar_prefetch=1,
            grid=(n // tc,),
            in_specs=[pl.BlockSpec((tc, d), lambda i, dest: (i, 0)),
                      pl.BlockSpec((tc, LANES), lambda i, dest: (i, 0)),
                      pl.BlockSpec((1, 1, d), lambda i, dest: (i // tpb, 0, 0)),
                      pl.BlockSpec((1, d), lambda i, dest: (0, 0)),
                      pl.BlockSpec(memory_space=pl.ANY)],
            out_specs=pl.BlockSpec((tc, d), lambda i, dest: (i, 0)),
            scratch_shapes=[pltpu.VMEM((2, TOP_K, tc * SUBLANES, LANES), F32), pltpu.SemaphoreType.DMA((2,))]),
        out_shape=jax.ShapeDtypeStruct((n, d), F32),
        compiler_params=_params("arbitrary"),
        name="combine",
    )(dest, h, route, gate2, fg, ys)


def _mixer(x2, mod, norm1_g, w_in, gla_wa2, gla_ba, gla_norm_g, cmp_pe, cmp_w1, cmp_w2, B, T):
    d = D_MODEL
    G = NSA_KV_HEADS
    hd = NSA_HD
    shift1 = mod[:, 0 * d:1 * d].reshape(B, 1, d)
    scale1 = mod[:, 1 * d:2 * d].reshape(B, 1, d)
    w_main = jnp.concatenate([w_in[:, :_O_GA], w_in[:, _O_NQ:_O_KV], w_in[:, _O_MG:_O_END],
                              w_in[:, _O_KV:_O_NG]], axis=1).astype(BF16)
    w_small = jnp.concatenate([w_in[:, _O_GA:_O_NQ], w_in[:, _O_NG:_O_MG],
                               jnp.zeros((d, LANES - GLA_GATE_RANK - NSA_HEADS * 3), F32)], axis=1).astype(BF16)
    main, small, kv32 = _inproj_call(x2, norm1_g.reshape(1, d), scale1, shift1, w_main, w_small, T)
    y_gla = _gla_call(main, small, gla_wa2, gla_ba.reshape(1, -1), gla_norm_g.reshape(1, -1), B, T)
    kvc = _compress_call(kv32, cmp_pe.reshape(2, 1, CMP_LEN * hd), cmp_w1.astype(BF16), cmp_w2.astype(BF16), B, T)
    o_cmp, sel = _cmpsel_call(main, kvc, B, T)
    y_nsa = _nsa_call(main, small, sel, o_cmp, B, T)
    return main, y_gla, y_nsa


def _route_tables(route, counts, n):
    m = MOE_ROWS
    e = route[:, TOP_K:2 * TOP_K].astype(jnp.int32)
    pos = route[:, 2 * TOP_K:3 * TOP_K].astype(jnp.int32)
    cnt = counts[0, :N_EXPERTS].astype(jnp.int32)
    padded = ((cnt + m - 1) // m) * m
    pends = jnp.cumsum(padded)
    pstarts = pends - padded
    onehot = e[..., None] == jnp.arange(N_EXPERTS, dtype=jnp.int32)
    dest = (jnp.sum(jnp.where(onehot, pstarts, 0), axis=-1) + pos).reshape(-1)
    p_rows = n * TOP_K + N_EXPERTS * m
    nblk = p_rows // m
    starts = jnp.arange(nblk, dtype=jnp.int32) * m
    blk_e = jnp.minimum(jnp.sum((pends[None, :] <= starts[:, None]).astype(jnp.int32), axis=1), N_EXPERTS - 1)
    nact = (pends[-1:] // m).astype(jnp.int32)
    return dest, pends.astype(jnp.int32), blk_e, nact, p_rows


def kernel(x, c, w_ada, b_ada, norm1_g, w_in, gla_wa2, gla_ba, gla_norm_g, cmp_pe, cmp_w1, cmp_w2, proj_gla,
           proj_nsa, w_out, norm2_g, router_w, router_b, moe_w1, moe_b1, moe_w2, moe_b2, final_g):
    B, T, d = x.shape
    n = B * T
    assert w_ada.shape[0] == 1 and d == D_MODEL and B <= 8, "single-layer block of width D_MODEL"
    l = 0
    h = x.reshape(n, d)
    c8 = jnp.concatenate([c, jnp.zeros((8 - B, d), c.dtype)], axis=0)
    mod = _mod_call(c8, w_ada[l], b_ada[l].reshape(1, -1))[:B]
    gate1 = mod[:, 2 * d:3 * d].reshape(B, 1, d)
    shift2 = mod[:, 3 * d:4 * d].reshape(B, 1, d)
    scale2 = mod[:, 4 * d:5 * d].reshape(B, 1, d)
    gate2 = mod[:, 5 * d:6 * d].reshape(B, 1, d)
    main, y_gla, y_nsa = _mixer(h, mod, norm1_g[l], w_in[l], gla_wa2[l], gla_ba[l], gla_norm_g[l],
                                cmp_pe[l], cmp_w1[l], cmp_w2[l], B, T)
    rw = jnp.concatenate([router_w[l], jnp.zeros((d, LANES - N_EXPERTS), F32)], axis=1)
    rb = jnp.concatenate([router_b[l], jnp.zeros((LANES - N_EXPERTS,), F32)]).reshape(1, LANES)
    h1, u2, route, counts = _merge_call(
        y_gla, y_nsa, main, h, gate1, scale2, shift2, norm2_g[l].reshape(1, d),
        proj_gla[l].astype(BF16), proj_nsa[l].astype(BF16), w_out[l].astype(BF16), rw, rb, T)
    dest, pends, blk_e, nact, p_rows = _route_tables(route, counts, n)
    xs = _dispatch_call(dest, pends, u2, p_rows)
    ys = _moe_call(blk_e, nact, xs, moe_w1[l], moe_b1[l].reshape(N_EXPERTS, 1, -1), moe_w2[l],
                   moe_b2[l].reshape(N_EXPERTS, 1, -1))
    out = _combine_call(dest, h1, route, gate2, final_g.reshape(1, d), ys, T)
    return out.reshape(B, T, d)
```

```python
import functools

import jax
import jax.numpy as jnp
from jax import lax
from jax.experimental import pallas as pl
from jax.experimental.pallas import tpu as pltpu

F32 = jnp.float32
BF16 = jnp.bfloat16

D_MODEL = 1024
GLA_HEADS = 4
GLA_DK = 128
GLA_DV = 256
GLA_GATE_RANK = 16
GLA_TAU = 16.0
GLA_CHUNK = 64
NSA_HEADS = 8
NSA_KV_HEADS = 2
NSA_HD = 128
NSA_GROUP = NSA_HEADS // NSA_KV_HEADS
CMP_LEN = 32
CMP_STRIDE = 16
CMP_HIDDEN = 2 * NSA_HD
SEL_BLOCK = 64
SEL_TOPK = 16
WIN = 512
N_EXPERTS = 32
TOP_K = 4
D_EXPERT = D_MODEL
SWIGLU_LIMIT = 7.0
SWIGLU_ALPHA = 1.702
RMS_EPS = 1e-5

LANES = 128
NEG = -1e30
LOG2E = 1.4426950408889634
MOE_ROWS = 512
VMEM_LIMIT = 56 * 1024 * 1024

C_GQ, C_GK, C_GV, C_GR, C_NQ, C_MG, C_KV = 0, 512, 1024, 2048, 3072, 4096, 6144
MAIN_COLS = 7680
CMP_KV_COLS = 2 * NSA_KV_HEADS * NSA_HD
S_GA, S_NG = 0, 16

_O_GA = 3072
_O_NQ = 3088
_O_KV = 4112
_O_NG = 5648
_O_MG = 5672
_O_END = 7720


def _dot(a, b):
    return jnp.dot(a, b, preferred_element_type=F32)


def _dot_nt(a, b):
    return lax.dot_general(a, b, (((1,), (1,)), ((), ())), preferred_element_type=F32)


def _dot_tn(a, b):
    return lax.dot_general(a, b, (((0,), (0,)), ((), ())), preferred_element_type=F32)


def _split(a):
    hi = a.astype(BF16)
    lo = (a - hi.astype(F32)).astype(BF16)
    return hi, lo


def _dot3(a, b):
    ah, al = _split(a)
    bh, bl = _split(b)
    return _dot(ah, bh) + _dot(ah, bl) + _dot(al, bh)


def _sigmoid(x):
    return 1.0 / (1.0 + jnp.exp(-x))


def _params(*sem):
    return pltpu.CompilerParams(dimension_semantics=sem, vmem_limit_bytes=VMEM_LIMIT)


def _mod_kernel(c_ref, w_ref, b_ref, o_ref):
    c = c_ref[...]
    cond = c * _sigmoid(c)
    o_ref[...] = _dot3(cond, w_ref[...]) + b_ref[...]


def _mod_call(c8, w_ada, b_ada):
    d = c8.shape[1]
    n = w_ada.shape[1]
    return pl.pallas_call(
        _mod_kernel,
        grid=(n // d,),
        in_specs=[pl.BlockSpec((8, d), lambda j: (0, 0)),
                  pl.BlockSpec((d, d), lambda j: (0, j)),
                  pl.BlockSpec((1, d), lambda j: (0, j))],
        out_specs=pl.BlockSpec((8, d), lambda j: (0, j)),
        out_shape=jax.ShapeDtypeStruct((8, n), F32),
        compiler_params=_params("parallel"),
        name="mod",
    )(c8, w_ada, b_ada)


def _inproj_kernel(x_ref, g_ref, sc_ref, sh_ref, w_ref, ws_ref, o_ref, s_ref, kv_ref, u_sc, *, kv_off):
    @pl.when(pl.program_id(1) == 0)
    def _():
        xf = x_ref[...]
        ms = jnp.mean(xf * xf, axis=-1, keepdims=True)
        y = xf * lax.rsqrt(ms + RMS_EPS) * g_ref[...]
        u = (y * (1.0 + sc_ref[0]) + sh_ref[0]).astype(BF16)
        u_sc[...] = u
        s_ref[...] = _dot(u, ws_ref[...])

    acc = _dot(u_sc[...], w_ref[...])
    o_ref[...] = acc.astype(BF16)

    @pl.when(pl.program_id(1) == pl.num_programs(1) - 1)
    def _():
        kv_ref[...] = acc[:, kv_off:kv_off + CMP_KV_COLS]


def _inproj_call(x2, g, scale, shift, w_main, w_small, T, tm=1024, tn=3840):
    n, d = x2.shape
    tpb = T // tm
    ncol = MAIN_COLS // tn
    kv_off = C_KV - (ncol - 1) * tn
    assert kv_off >= 0 and kv_off % LANES == 0 and kv_off + CMP_KV_COLS <= tn
    return pl.pallas_call(
        functools.partial(_inproj_kernel, kv_off=kv_off),
        grid=(n // tm, ncol),
        in_specs=[pl.BlockSpec((tm, d), lambda i, j: (i, 0)),
                  pl.BlockSpec((1, d), lambda i, j: (0, 0)),
                  pl.BlockSpec((1, 1, d), lambda i, j: (i // tpb, 0, 0)),
                  pl.BlockSpec((1, 1, d), lambda i, j: (i // tpb, 0, 0)),
                  pl.BlockSpec((d, tn), lambda i, j: (0, j)),
                  pl.BlockSpec((d, LANES), lambda i, j: (0, 0))],
        out_specs=[pl.BlockSpec((tm, tn), lambda i, j: (i, j)),
                   pl.BlockSpec((tm, LANES), lambda i, j: (i, 0)),
                   pl.BlockSpec((tm, CMP_KV_COLS), lambda i, j: (i, 0))],
        out_shape=[jax.ShapeDtypeStruct((n, MAIN_COLS), BF16),
                   jax.ShapeDtypeStruct((n, LANES), F32),
                   jax.ShapeDtypeStruct((n, CMP_KV_COLS), F32)],
        scratch_shapes=[pltpu.VMEM((tm, d), BF16)],
        compiler_params=_params("parallel", "arbitrary"),
        name="inproj",
    )(x2, g, scale, shift, w_main, w_small)


def _gla_kernel(q_ref, k_ref, v_ref, r_ref, s_ref, wa_ref, ba_ref, g_ref, o_ref, st_sc, *, tt):
    @pl.when(pl.program_id(1) == 0)
    def _():
        st_sc[...] = jnp.zeros_like(st_sc)

    C = GLA_CHUNK
    nc = tt // C
    dk, dv = GLA_DK, GLA_DV
    ga = s_ref[:, S_GA:S_GA + GLA_GATE_RANK]
    z = _dot3(ga, wa_ref[...]) + ba_ref[...]
    la = (jnp.minimum(z, 0.0) - jnp.log(1.0 + jnp.exp(-jnp.abs(z)))) * (1.0 / GLA_TAU)
    row = lax.broadcasted_iota(jnp.int32, (tt, tt), 0)
    col = lax.broadcasted_iota(jnp.int32, (tt, tt), 1)
    causal = ((row // C) == (col // C)) & (col <= row)
    tril = jnp.where(causal, 1.0, 0.0).astype(BF16)
    hi, lo = _split(la)
    bc_all = _dot(tril, hi) + _dot(tril, lo)
    for h in range(GLA_HEADS):
        bc = bc_all[:, h * dk:(h + 1) * dk]
        bls = [bc[(c + 1) * C - 1:(c + 1) * C, :] for c in range(nc)]
        blr = jnp.concatenate([jnp.broadcast_to(b, (C, dk)) for b in bls], axis=0)
        q = q_ref[:, h * dk:(h + 1) * dk].astype(F32) * (dk ** -0.5)
        k = k_ref[:, h * dk:(h + 1) * dk].astype(F32)
        v = v_ref[:, h * dv:(h + 1) * dv]
        qe = (q * jnp.exp(bc)).astype(BF16)
        ke = (k * jnp.exp(-bc)).astype(BF16)
        kd = (k * jnp.exp(blr - bc)).astype(BF16)
        a = jnp.where(causal, _dot_nt(qe, ke), 0.0)
        intra = _dot(a.astype(BF16), v)
        st = st_sc[h]
        inter = []
        for c in range(nc):
            sl = slice(c * C, (c + 1) * C)
            inter.append(_dot_nt(qe[sl], st.astype(BF16)))
            st = st * jnp.exp(bls[c]) + _dot_tn(v[sl], kd[sl])
        st_sc[h] = st
        o = intra + jnp.concatenate(inter, axis=0)
        y = o * lax.rsqrt(jnp.mean(o * o, axis=-1, keepdims=True) + RMS_EPS) * g_ref[...]
        r = r_ref[:, h * dv:(h + 1) * dv].astype(F32)
        o_ref[:, h * dv:(h + 1) * dv] = (y * (r * _sigmoid(r))).astype(BF16)


def _gla_call(main, small, wa2, ba, norm_g, B, T, tt=256):
    n = main.shape[0]
    nt = T // tt
    H = GLA_HEADS
    rowmap = lambda b, t: b * nt + t
    return pl.pallas_call(
        functools.partial(_gla_kernel, tt=tt),
        grid=(B, nt),
        in_specs=[pl.BlockSpec((tt, H * GLA_DK), lambda b, t: (rowmap(b, t), C_GQ // (H * GLA_DK))),
                  pl.BlockSpec((tt, H * GLA_DK), lambda b, t: (rowmap(b, t), C_GK // (H * GLA_DK))),
                  pl.BlockSpec((tt, H * GLA_DV), lambda b, t: (rowmap(b, t), C_GV // (H * GLA_DV))),
                  pl.BlockSpec((tt, H * GLA_DV), lambda b, t: (rowmap(b, t), C_GR // (H * GLA_DV))),
                  pl.BlockSpec((tt, LANES), lambda b, t: (rowmap(b, t), 0)),
                  pl.BlockSpec((GLA_GATE_RANK, H * GLA_DK), lambda b, t: (0, 0)),
                  pl.BlockSpec((1, H * GLA_DK), lambda b, t: (0, 0)),
                  pl.BlockSpec((1, GLA_DV), lambda b, t: (0, 0))],
        out_specs=pl.BlockSpec((tt, H * GLA_DV), lambda b, t: (rowmap(b, t), 0)),
        out_shape=jax.ShapeDtypeStruct((n, H * GLA_DV), BF16),
        scratch_shapes=[pltpu.VMEM((H, GLA_DV, GLA_DK), F32)],
        compiler_params=_params("parallel", "arbitrary"),
        name="gla",
    )(main, main, main, main, small, wa2, ba, norm_g)


def _compress_kernel(x_ref, pe_ref, w1_ref, w2_ref, o_ref):
    half = CMP_STRIDE * NSA_HD
    nb = x_ref.shape[0] // CMP_STRIDE
    x = jnp.concatenate([x_ref[pl.ds(t, nb, stride=CMP_STRIDE), :].astype(BF16) for t in range(CMP_STRIDE)], axis=1)
    ylo = _dot(x, w1_ref[0, :half, :])
    yhi = _dot(x, w1_ref[0, half:, :])
    pe = jnp.broadcast_to(pe_ref[0], (8, 2 * half)).astype(BF16)
    pterm = _dot(pe, w1_ref[0])[0:1, :]
    pre = ylo + pltpu.roll(yhi, nb - 1, 0) + pterm
    hcu = pre * pre * pre
    hid = 0.5 * pre * (1.0 + jnp.tanh(0.7978845608028654 * (pre + 0.044715 * hcu)))
    o_ref[0, 0] = _dot(hid.astype(BF16), w2_ref[0]).astype(BF16)


def _compress_call(kv32, pe, w1, w2, B, T):
    G = NSA_KV_HEADS
    bg = B * G
    nb = T // CMP_STRIDE
    return pl.pallas_call(
        _compress_kernel,
        grid=(2, bg),
        in_specs=[pl.BlockSpec((T, NSA_HD), lambda s, i: (i // G, s * G + i % G)),
                  pl.BlockSpec((1, 1, CMP_LEN * NSA_HD), lambda s, i: (s, 0, 0)),
                  pl.BlockSpec((1, CMP_LEN * NSA_HD, CMP_HIDDEN), lambda s, i: (s, 0, 0)),
                  pl.BlockSpec((1, CMP_HIDDEN, NSA_HD), lambda s, i: (s, 0, 0))],
        out_specs=pl.BlockSpec((1, 1, nb, NSA_HD), lambda s, i: (s, i, 0, 0)),
        out_shape=jax.ShapeDtypeStruct((2, bg, nb, NSA_HD), BF16),
        compiler_params=_params("parallel", "parallel"),
        name="compress",
    )(kv32, pe, w1, w2)


def _cmpsel_kernel(q_ref, kc_ref, vc_ref, o_ref, sel_ref, *, tq, nsel):
    t0 = pl.program_id(2) * tq
    kc = kc_ref[0, 0]
    vc = vc_ref[0, 0]
    ncp = kc.shape[0]
    tpos = t0 + lax.broadcasted_iota(jnp.int32, (tq, 1), 0)
    cidx = lax.broadcasted_iota(jnp.int32, (1, ncp), 1)
    cmask = (cidx * CMP_STRIDE + (CMP_LEN - 1)) <= tpos
    psum = jnp.zeros((tq, ncp), F32)
    for r in range(NSA_GROUP):
        q = q_ref[:, r * NSA_HD:(r + 1) * NSA_HD]
        s = jnp.where(cmask, _dot_nt(q, kc) * (NSA_HD ** -0.5), NEG)
        m = jnp.max(s, axis=-1, keepdims=True)
        e = jnp.where(cmask, jnp.exp(s - m), 0.0)
        p = e / jnp.maximum(jnp.sum(e, axis=-1, keepdims=True), 1e-30)
        o_ref[:, r * NSA_HD:(r + 1) * NSA_HD] = _dot(p.astype(BF16), vc).astype(BF16)
        psum = psum + p
    ci = lax.broadcasted_iota(jnp.int32, (nsel, ncp), 1) * CMP_STRIDE
    si = lax.broadcasted_iota(jnp.int32, (nsel, ncp), 0) * SEL_BLOCK
    ov = jnp.where((ci < si + SEL_BLOCK) & (ci + CMP_LEN > si), 1.0, 0.0).astype(BF16)
    hi, lo = _split(psum)
    imp = _dot_nt(ov, hi) + _dot_nt(ov, lo)
    tlane = t0 + lax.broadcasted_iota(jnp.int32, (1, tq), 1)
    jblk = lax.broadcasted_iota(jnp.int32, (nsel, 1), 0)
    tblk = tlane // SEL_BLOCK
    forced = (jblk == 0) | (jblk == tblk) | (jblk == tblk - 1)
    valid = (jblk * SEL_BLOCK) <= tlane
    score = jnp.where(forced, jnp.inf, jnp.where(valid, imp, -jnp.inf))
    ngrp = nsel // 8
    groups = [score[8 * a:8 * a + 8, :] for a in range(ngrp)]
    sub = lax.broadcasted_iota(jnp.int32, (8, 1), 0)
    ranks = [jnp.zeros((8, tq), F32) for _ in range(ngrp)]
    for i in range(nsel):
        si_ = score[i:i + 1, :]
        for a in range(ngrp):
            if a < i // 8:
                beats = si_ > groups[a]
            elif a > i // 8:
                beats = si_ >= groups[a]
            else:
                beats = (si_ > groups[a]) | ((si_ == groups[a]) & (sub > i % 8))
            ranks[a] = ranks[a] + jnp.where(beats, 1.0, 0.0)
    rank = jnp.concatenate(ranks, axis=0)
    ntop = min(SEL_TOPK, nsel)
    chosen = jnp.where(valid & (rank < ntop), 1.0, 0.0)
    sel_ref[0, 0] = jnp.transpose(chosen).astype(BF16)


def _cmpsel_call(main, kvc, B, T, tq=512):
    n = main.shape[0]
    nq = T // tq
    nsel = T // SEL_BLOCK
    ncp = kvc.shape[2]
    G = NSA_KV_HEADS
    qw = NSA_GROUP * NSA_HD
    return pl.pallas_call(
        functools.partial(_cmpsel_kernel, tq=tq, nsel=nsel),
        grid=(B, G, nq),
        in_specs=[pl.BlockSpec((tq, qw), lambda b, g, t: (b * nq + t, C_NQ // qw + g)),
                  pl.BlockSpec((1, 1, ncp, NSA_HD), lambda b, g, t: (0, b * G + g, 0, 0)),
                  pl.BlockSpec((1, 1, ncp, NSA_HD), lambda b, g, t: (1, b * G + g, 0, 0))],
        out_specs=[pl.BlockSpec((tq, qw), lambda b, g, t: (b * nq + t, g)),
                   pl.BlockSpec((1, 1, tq, nsel), lambda b, g, t: (b, g, t, 0))],
        out_shape=[jax.ShapeDtypeStruct((n, NSA_HEADS * NSA_HD), BF16),
                   jax.ShapeDtypeStruct((B, G, T, nsel), BF16)],
        compiler_params=_params("parallel", "parallel", "parallel"),
        name="cmpsel",
    )(main, kvc, kvc)


def _nsa_kernel(q_ref, ks_ref, vs_ref, kw_ref, vw_ref, sel_ref, oc_ref, s_ref, o_ref, s_sc, *, tq, tk, T, nsel):
    R = NSA_GROUP
    hd = NSA_HD
    g = pl.program_id(1)
    t0 = pl.program_id(2) * tq
    qscale = (hd ** -0.5) * LOG2E
    q4 = jnp.concatenate([q_ref[:, r * hd:(r + 1) * hd] for r in range(R)], axis=0)
    q4 = (q4.astype(F32) * qscale).astype(BF16)
    rowpos = t0 + lax.broadcasted_iota(jnp.int32, (tq, 1), 0)
    selm = sel_ref[0, 0].astype(F32) - 1.0
    selm4 = jnp.concatenate([selm] * R, axis=0).astype(BF16)

    qa = jnp.concatenate([q4, selm4], axis=1)
    jcol = lax.broadcasted_iota(jnp.int32, (tk, nsel), 1)
    crow = lax.broadcasted_iota(jnp.int32, (tk, nsel), 0) // SEL_BLOCK
    kcol = lax.broadcasted_iota(jnp.int32, (1, tk), 1)

    def scores(kt):
        k0 = pl.multiple_of(kt * tk, tk)
        ind = jnp.where(jcol == crow + kt * (tk // SEL_BLOCK), 1e30, 0.0).astype(BF16)
        s_sc[kt % 2] = _dot_nt(qa, jnp.concatenate([ks_ref[pl.ds(k0, tk), :], ind], axis=1))

    def attend(kt, carry, diagonal):
        m, l, acc = carry
        k0 = pl.multiple_of(kt * tk, tk)
        s = s_sc[kt % 2].reshape(R, tq, tk)
        if diagonal:
            s = s + jnp.where((k0 + kcol) <= rowpos, 0.0, NEG)[None]
        m_new = jnp.maximum(m, jnp.max(s, axis=-1, keepdims=True))
        alpha = jnp.exp2(m - m_new)
        p = jnp.exp2(s - m_new)
        l = alpha * l + jnp.sum(p, axis=-1, keepdims=True)
        pv = _dot(p.reshape(R * tq, tk).astype(BF16), vs_ref[pl.ds(k0, tk), :]).reshape(R, tq, hd)
        return m_new, l, alpha * acc + pv

    def step(kt, carry):
        carry = attend(kt, carry, False)
        scores(kt + 1)
        return carry

    init = (jnp.full((R, tq, 1), NEG, F32), jnp.zeros((R, tq, 1), F32), jnp.zeros((R, tq, hd), F32))
    kdiag = t0 // tk
    scores(0)
    carry = lax.fori_loop(0, kdiag, step, init)
    _, l_s, acc_s = attend(kdiag, carry, True)
    o_sel = acc_s / l_s

    wlen = WIN + tq
    w0 = pl.multiple_of(jnp.maximum(t0 - WIN, 0), tq)
    kw = kw_ref[pl.ds(w0, wlen), :]
    vw = vw_ref[pl.ds(w0, wlen), :]
    kpos = w0 + lax.broadcasted_iota(jnp.int32, (1, wlen), 1)
    wbias = jnp.where((kpos <= rowpos) & (rowpos - kpos < WIN), 0.0, NEG)
    sw = _dot_nt(q4, kw).reshape(R, tq, wlen) + wbias[None]
    mw = jnp.max(sw, axis=-1, keepdims=True)
    pw = jnp.exp2(sw - mw)
    lw = jnp.sum(pw, axis=-1, keepdims=True)
    o_win = _dot(pw.reshape(R * tq, wlen).astype(BF16), vw).reshape(R, tq, hd) / lw

    ng = R * 3
    graw = jnp.where(g == 0, s_ref[:, S_NG:S_NG + ng], s_ref[:, S_NG + ng:S_NG + 2 * ng])
    gates = _sigmoid(graw)
    for r in range(R):
        oc = oc_ref[:, r * hd:(r + 1) * hd].astype(F32)
        out = (gates[:, 3 * r:3 * r + 1] * oc + gates[:, 3 * r + 1:3 * r + 2] * o_sel[r]
               + gates[:, 3 * r + 2:3 * r + 3] * o_win[r])
        o_ref[:, r * hd:(r + 1) * hd] = out.astype(BF16)


def _nsa_call(main, small, sel, o_cmp, B, T, tq=128, tk=1024):
    n = main.shape[0]
    nq = T // tq
    nsel = T // SEL_BLOCK
    G = NSA_KV_HEADS
    hd = NSA_HD
    qw = NSA_GROUP * hd
    tk = min(tk, T)
    kvb = C_KV // hd

    def kvspec(slot):
        return pl.BlockSpec((T, hd), lambda b, g, t: (b, kvb + 2 * slot + g))

    return pl.pallas_call(
        functools.partial(_nsa_kernel, tq=tq, tk=tk, T=T, nsel=nsel),
        grid=(B, G, nq),
        in_specs=[pl.BlockSpec((tq, qw), lambda b, g, t: (b * nq + t, C_NQ // qw + g)),
                  kvspec(2), kvspec(3), kvspec(4), kvspec(5),
                  pl.BlockSpec((1, 1, tq, nsel), lambda b, g, t: (b, g, t, 0)),
                  pl.BlockSpec((tq, qw), lambda b, g, t: (b * nq + t, g)),
                  pl.BlockSpec((tq, LANES), lambda b, g, t: (b * nq + t, 0))],
        out_specs=pl.BlockSpec((tq, qw), lambda b, g, t: (b * nq + t, g)),
        out_shape=jax.ShapeDtypeStruct((n, NSA_HEADS * hd), BF16),
        scratch_shapes=[pltpu.VMEM((2, NSA_GROUP * tq, tk), F32)],
        compiler_params=_params("parallel", "parallel", "parallel"),
        name="nsa",
    )(main, main, main, main, main, sel, o_cmp, small)


def _merge_kernel(yg_ref, yn_ref, mg_ref, x_ref, g1_ref, sc_ref, sh_ref, n2_ref, pg_ref, pn_ref, wo_ref,
                  rw_ref, rb_ref, h_ref, u_ref, rt_ref, cnt_ref, carry_sc, *, tm):
    d = D_MODEL

    @pl.when(pl.program_id(0) == 0)
    def _():
        carry_sc[...] = jnp.zeros_like(carry_sc)

    a = _dot(yg_ref[...], pg_ref[...])
    b = _dot(yn_ref[...], pn_ref[...])
    mg = _sigmoid(mg_ref[...].astype(F32))
    merged = mg[:, :d] * a + mg[:, d:] * b
    mix = _dot(merged.astype(BF16), wo_ref[...])
    h = x_ref[...] + g1_ref[0] * mix
    h_ref[...] = h
    y = h * lax.rsqrt(jnp.mean(h * h, axis=-1, keepdims=True) + RMS_EPS) * n2_ref[...]
    u = y * (1.0 + sc_ref[0]) + sh_ref[0]
    u_ref[...] = u

    lane = lax.broadcasted_iota(jnp.int32, (1, LANES), 1).astype(F32)
    logits = jnp.where(lane < N_EXPERTS, _dot3(u, rw_ref[...]) + rb_ref[...], -jnp.inf)
    vals, idxs = [], []
    cur = logits
    for _ in range(TOP_K):
        m = jnp.max(cur, axis=-1, keepdims=True)
        idx = jnp.min(jnp.where(cur == m, lane, float(LANES)), axis=-1, keepdims=True)
        vals.append(m)
        idxs.append(idx)
        cur = jnp.where(lane == idx, -jnp.inf, cur)
    es = [jnp.exp(v - vals[0]) for v in vals]
    den = es[0] + es[1] + es[2] + es[3]
    onehots = [lane == idx for idx in idxs]
    oh = jnp.zeros((tm, LANES), F32)
    for o in onehots:
        oh = oh + jnp.where(o, 1.0, 0.0)
    row = lax.broadcasted_iota(jnp.int32, (tm, tm), 0)
    col = lax.broadcasted_iota(jnp.int32, (tm, tm), 1)
    tril = jnp.where(col <= row, 1.0, 0.0).astype(BF16)
    cnt = _dot(tril, oh.astype(BF16))
    tot = cnt + carry_sc[0:1, :]
    route = jnp.zeros((tm, LANES), F32)
    for k in range(TOP_K):
        pos = jnp.sum(jnp.where(onehots[k], tot - 1.0, 0.0), axis=-1, keepdims=True)
        route = jnp.where(lane == k, es[k] / den, route)
        route = jnp.where(lane == TOP_K + k, idxs[k], route)
        route = jnp.where(lane == 2 * TOP_K + k, pos, route)
    rt_ref[...] = route
    newc = tot[tm - 1:tm, :]
    carry_sc[...] = jnp.broadcast_to(newc, carry_sc.shape)
    cnt_ref[...] = jnp.broadcast_to(newc, cnt_ref.shape)


def _merge_call(y_gla, y_nsa, main, x2, gate1, scale2, shift2, n2g, pg, pn, wo, rw, rb, T, tm=512):
    n, d = x2.shape
    tpb = T // tm
    bmap = lambda i: (i // tpb, 0, 0)
    full = lambda i: (0, 0)
    return pl.pallas_call(
        functools.partial(_merge_kernel, tm=tm),
        grid=(n // tm,),
        in_specs=[pl.BlockSpec((tm, d), lambda i: (i, 0)),
                  pl.BlockSpec((tm, d), lambda i: (i, 0)),
                  pl.BlockSpec((tm, 2 * d), lambda i: (i, C_MG // (2 * d))),
                  pl.BlockSpec((tm, d), lambda i: (i, 0)),
                  pl.BlockSpec((1, 1, d), bmap), pl.BlockSpec((1, 1, d), bmap), pl.BlockSpec((1, 1, d), bmap),
                  pl.BlockSpec((1, d), full),
                  pl.BlockSpec((d, d), full), pl.BlockSpec((d, d), full), pl.BlockSpec((d, d), full),
                  pl.BlockSpec((d, LANES), full), pl.BlockSpec((1, LANES), full)],
        out_specs=[pl.BlockSpec((tm, d), lambda i: (i, 0)),
                   pl.BlockSpec((tm, d), lambda i: (i, 0)),
                   pl.BlockSpec((tm, LANES), lambda i: (i, 0)),
                   pl.BlockSpec((8, LANES), full)],
        out_shape=[jax.ShapeDtypeStruct((n, d), F32), jax.ShapeDtypeStruct((n, d), F32),
                   jax.ShapeDtypeStruct((n, LANES), F32), jax.ShapeDtypeStruct((8, LANES), F32)],
        scratch_shapes=[pltpu.VMEM((8, LANES), F32)],
        compiler_params=_params("arbitrary"),
        name="merge",
    )(y_gla, y_nsa, main, x2, gate1, scale2, shift2, n2g, pg, pn, wo, rw, rb)


SUBLANES = 8
ROW_TILES = D_MODEL // LANES
assert ROW_TILES == SUBLANES


def _to_token_tiles(dst_ref, src, rows):
    for j in range(ROW_TILES):
        dst_ref[pl.ds(j, rows, stride=SUBLANES), :] = src[:, j * LANES:(j + 1) * LANES]


def _from_token_tiles(src_ref, rows):
    return [src_ref[pl.ds(j, rows, stride=SUBLANES), :] for j in range(ROW_TILES)]


def _dispatch_kernel(dest_ref, pend_ref, u_ref, xs_ref, xt, zbuf, sem, zsem, *, td):
    base = pl.program_id(0) * (td * TOP_K)
    blk = MOE_ROWS * SUBLANES

    @pl.when(pl.program_id(0) == 0)
    def _():
        zbuf[...] = jnp.zeros_like(zbuf)

        def last_block(e):
            end = pend_ref[e]
            prev = pend_ref[e - 1] if e else 0
            start = pl.multiple_of(jnp.maximum(end - MOE_ROWS, 0) * SUBLANES, blk)
            return end > prev, pltpu.make_async_copy(zbuf, xs_ref.at[pl.ds(start, blk)], zsem)

        def unused_block(b):
            return b * MOE_ROWS >= pend_ref[N_EXPERTS - 1], pltpu.make_async_copy(
                zbuf, xs_ref.at[pl.ds(b * blk, blk)], zsem)

        nblk = xs_ref.shape[0] // blk
        fills = [functools.partial(last_block, e) for e in range(N_EXPERTS)]
        fills += [functools.partial(unused_block, b) for b in range(nblk - N_EXPERTS, nblk)]
        for f in fills:
            cond, fill = f()
            pl.when(cond)(fill.start)
        for f in fills:
            cond, fill = f()
            pl.when(cond)(fill.wait)

    _to_token_tiles(xt, u_ref[...], td)

    def row_copy(r, slot):
        return pltpu.make_async_copy(xt.at[pl.ds(pl.multiple_of(r * SUBLANES, SUBLANES), SUBLANES)],
                                     xs_ref.at[pl.ds(pl.multiple_of(slot * SUBLANES, SUBLANES), SUBLANES)], sem)

    def issue(r, c):
        for k in range(TOP_K):
            row_copy(r, dest_ref[base + r * TOP_K + k]).start(priority=k % 2)
        return c

    def drain(r, c):
        for k in range(TOP_K):
            row_copy(r, dest_ref[base + r * TOP_K + k]).wait()
        return c

    lax.fori_loop(0, td, issue, 0)
    lax.fori_loop(0, td, drain, 0)


def _dispatch_call(dest, pends, u2, p_rows, td=512):
    n, d = u2.shape
    return pl.pallas_call(
        functools.partial(_dispatch_kernel, td=td),
        grid_spec=pltpu.PrefetchScalarGridSpec(
            num_scalar_prefetch=2,
            grid=(n // td,),
            in_specs=[pl.BlockSpec((td, d), lambda i, dest, pends: (i, 0))],
            out_specs=pl.BlockSpec(memory_space=pl.ANY),
            scratch_shapes=[pltpu.VMEM((td * SUBLANES, LANES), F32), pltpu.VMEM((MOE_ROWS * SUBLANES, LANES), F32),
                            pltpu.SemaphoreType.DMA(()), pltpu.SemaphoreType.DMA(())]),
        out_shape=jax.ShapeDtypeStruct((p_rows * SUBLANES, LANES), F32),
        compiler_params=_params("arbitrary"),
        name="dispatch",
    )(dest, pends, u2)


def _moe_kernel(be_ref, na_ref, x_ref, w1_ref, b1_ref, w2_ref, b2_ref, y_ref, w1b, w2b):
    i = pl.program_id(0)
    e = be_ref[i]
    prev = be_ref[jnp.maximum(i - 1, 0)]
    active = i < na_ref[0]
    f = D_EXPERT
    m = MOE_ROWS

    @pl.when(active & ((i == 0) | (e != prev)))
    def _():
        w1b[...] = w1_ref[0].astype(BF16)
        w2b[...] = w2_ref[0].astype(BF16)

    @pl.when(active)
    def _():
        x = jnp.concatenate([t.astype(BF16) for t in _from_token_tiles(x_ref, m)], axis=1)
        hb = _dot(x, w1b[...]) + b1_ref[0]
        gate = jnp.minimum(hb[:, :f], SWIGLU_LIMIT)
        up = jnp.clip(hb[:, f:], -SWIGLU_LIMIT, SWIGLU_LIMIT)
        act = gate * _sigmoid(SWIGLU_ALPHA * gate) * (up + 1.0)
        _to_token_tiles(y_ref, _dot(act.astype(BF16), w2b[...]) + b2_ref[0], m)

    @pl.when(jnp.logical_not(active))
    def _():
        y_ref[...] = jnp.zeros_like(y_ref)


def _moe_call(blk_e, nact, xs, w1, b1, w2, b2):
    d, f2 = w1.shape[1], w1.shape[2]
    blk = MOE_ROWS * SUBLANES
    return pl.pallas_call(
        _moe_kernel,
        grid_spec=pltpu.PrefetchScalarGridSpec(
            num_scalar_prefetch=2,
            grid=(xs.shape[0] // blk,),
            in_specs=[pl.BlockSpec((blk, LANES), lambda i, be, na: (i, 0)),
                      pl.BlockSpec((1, d, f2), lambda i, be, na: (be[i], 0, 0)),
                      pl.BlockSpec((1, 1, f2), lambda i, be, na: (be[i], 0, 0)),
                      pl.BlockSpec((1, f2 // 2, d), lambda i, be, na: (be[i], 0, 0)),
                      pl.BlockSpec((1, 1, d), lambda i, be, na: (be[i], 0, 0))],
            out_specs=pl.BlockSpec((blk, LANES), lambda i, be, na: (i, 0)),
            scratch_shapes=[pltpu.VMEM((d, f2), BF16), pltpu.VMEM((f2 // 2, d), BF16)]),
        out_shape=jax.ShapeDtypeStruct(xs.shape, F32),
        compiler_params=_params("arbitrary"),
        name="moe",
    )(blk_e, nact, xs, w1, b1, w2, b2)


def _combine_kernel(dest_ref, h_ref, rt_ref, g2_ref, fg_ref, ys_ref, o_ref, buf, sems, *, tc):
    i = pl.program_id(0)
    slot = i % 2

    def row_copy(step, sl, r, k):
        src = dest_ref[step * (tc * TOP_K) + r * TOP_K + k]
        return pltpu.make_async_copy(
            ys_ref.at[pl.ds(pl.multiple_of(src * SUBLANES, SUBLANES), SUBLANES)],
            buf.at[sl, k, pl.ds(pl.multiple_of(r * SUBLANES, SUBLANES), SUBLANES)], sems.at[sl])

    def issue(step, sl):
        def body(r, c):
            for k in range(TOP_K):
                row_copy(step, sl, r, k).start(priority=k % 2)
            return c
        lax.fori_loop(0, tc, body, 0)

    def drain(step, sl):
        def body(r, c):
            for k in range(TOP_K):
                row_copy(step, sl, r, k).wait()
            return c
        lax.fori_loop(0, tc, body, 0)

    @pl.when(i == 0)
    def _():
        issue(0, 0)

    @pl.when(i + 1 < pl.num_programs(0))
    def _():
        issue(i + 1, 1 - slot)

    drain(i, slot)
    rt = rt_ref[...]
    slabs = [_from_token_tiles(buf.at[slot, k], tc) for k in range(TOP_K)]
    cols = []
    for j in range(ROW_TILES):
        col = rt[:, 0:1] * slabs[0][j]
        for k in range(1, TOP_K):
            col = col + rt[:, k:k + 1] * slabs[k][j]
        cols.append(col)
    moe = jnp.concatenate(cols, axis=1)
    hh = h_ref[...] + g2_ref[0] * moe
    o_ref[...] = hh * lax.rsqrt(jnp.mean(hh * hh, axis=-1, keepdims=True) + RMS_EPS) * fg_ref[...]


def _combine_call(dest, h, route, gate2, fg, ys, T, tc=256):
    n, d = h.shape
    tpb = T // tc
    return pl.pallas_call(
        functools.partial(_combine_kernel, tc=tc),
        grid_spec=pltpu.PrefetchScalarGridSpec(
            num_scalar_prefetch=1,
            grid=(n // tc,),
            in_specs=[pl.BlockSpec((tc, d), lambda i, dest: (i, 0)),
                      pl.BlockSpec((tc, LANES), lambda i, dest: (i, 0)),
                      pl.BlockSpec((1, 1, d), lambda i, dest: (i // tpb, 0, 0)),
                      pl.BlockSpec((1, d), lambda i, dest: (0, 0)),
                      pl.BlockSpec(memory_space=pl.ANY)],
            out_specs=pl.BlockSpec((tc, d), lambda i, dest: (i, 0)),
            scratch_shapes=[pltpu.VMEM((2, TOP_K, tc * SUBLANES, LANES), F32), pltpu.SemaphoreType.DMA((2,))]),
        out_shape=jax.ShapeDtypeStruct((n, d), F32),
        compiler_params=_params("arbitrary"),
        name="combine",
    )(dest, h, route, gate2, fg, ys)


def _mixer(x2, mod, norm1_g, w_in, gla_wa2, gla_ba, gla_norm_g, cmp_pe, cmp_w1, cmp_w2, B, T):
    d = D_MODEL
    G = NSA_KV_HEADS
    hd = NSA_HD
    shift1 = mod[:, 0 * d:1 * d].reshape(B, 1, d)
    scale1 = mod[:, 1 * d:2 * d].reshape(B, 1, d)
    w_main = jnp.concatenate([w_in[:, :_O_GA], w_in[:, _O_NQ:_O_KV], w_in[:, _O_MG:_O_END],
                              w_in[:, _O_KV:_O_NG]], axis=1).astype(BF16)
    w_small = jnp.concatenate([w_in[:, _O_GA:_O_NQ], w_in[:, _O_NG:_O_MG],
                               jnp.zeros((d, LANES - GLA_GATE_RANK - NSA_HEADS * 3), F32)], axis=1).astype(BF16)
    main, small, kv32 = _inproj_call(x2, norm1_g.reshape(1, d), scale1, shift1, w_main, w_small, T)
    y_gla = _gla_call(main, small, gla_wa2, gla_ba.reshape(1, -1), gla_norm_g.reshape(1, -1), B, T)
    kvc = _compress_call(kv32, cmp_pe.reshape(2, 1, CMP_LEN * hd), cmp_w1.astype(BF16), cmp_w2.astype(BF16), B, T)
    o_cmp, sel = _cmpsel_call(main, kvc, B, T)
    y_nsa = _nsa_call(main, small, sel, o_cmp, B, T)
    return main, y_gla, y_nsa


def _route_tables(route, counts, n):
    m = MOE_ROWS
    e = route[:, TOP_K:2 * TOP_K].astype(jnp.int32)
    pos = route[:, 2 * TOP_K:3 * TOP_K].astype(jnp.int32)
    cnt = counts[0, :N_EXPERTS].astype(jnp.int32)
    padded = ((cnt + m - 1) // m) * m
    pends = jnp.cumsum(padded)
    pstarts = pends - padded
    onehot = e[..., None] == jnp.arange(N_EXPERTS, dtype=jnp.int32)
    dest = (jnp.sum(jnp.where(onehot, pstarts, 0), axis=-1) + pos).reshape(-1)
    p_rows = n * TOP_K + N_EXPERTS * m
    nblk = p_rows // m
    starts = jnp.arange(nblk, dtype=jnp.int32) * m
    blk_e = jnp.minimum(jnp.sum((pends[None, :] <= starts[:, None]).astype(jnp.int32), axis=1), N_EXPERTS - 1)
    nact = (pends[-1:] // m).astype(jnp.int32)
    return dest, pends.astype(jnp.int32), blk_e, nact, p_rows


def kernel(x, c, w_ada, b_ada, norm1_g, w_in, gla_wa2, gla_ba, gla_norm_g, cmp_pe, cmp_w1, cmp_w2, proj_gla,
           proj_nsa, w_out, norm2_g, router_w, router_b, moe_w1, moe_b1, moe_w2, moe_b2, final_g):
    B, T, d = x.shape
    n = B * T
    assert w_ada.shape[0] == 1 and d == D_MODEL and B <= 8, "single-layer block of width D_MODEL"
    l = 0
    h = x.reshape(n, d)
    c8 = jnp.concatenate([c, jnp.zeros((8 - B, d), c.dtype)], axis=0)
    mod = _mod_call(c8, w_ada[l], b_ada[l].reshape(1, -1))[:B]
    gate1 = mod[:, 2 * d:3 * d].reshape(B, 1, d)
    shift2 = mod[:, 3 * d:4 * d].reshape(B, 1, d)
    scale2 = mod[:, 4 * d:5 * d].reshape(B, 1, d)
    gate2 = mod[:, 5 * d:6 * d].reshape(B, 1, d)
    main, y_gla, y_nsa = _mixer(h, mod, norm1_g[l], w_in[l], gla_wa2[l], gla_ba[l], gla_norm_g[l],
                                cmp_pe[l], cmp_w1[l], cmp_w2[l], B, T)
    rw = jnp.concatenate([router_w[l], jnp.zeros((d, LANES - N_EXPERTS), F32)], axis=1)
    rb = jnp.concatenate([router_b[l], jnp.zeros((LANES - N_EXPERTS,), F32)]).reshape(1, LANES)
    h1, u2, route, counts = _merge_call(
        y_gla, y_nsa, main, h, gate1, scale2, shift2, norm2_g[l].reshape(1, d),
        proj_gla[l].astype(BF16), proj_nsa[l].astype(BF16), w_out[l].astype(BF16), rw, rb, T)
    dest, pends, blk_e, nact, p_rows = _route_tables(route, counts, n)
    xs = _dispatch_call(dest, pends, u2, p_rows)
    ys = _moe_call(blk_e, nact, xs, moe_w1[l], moe_b1[l].reshape(N_EXPERTS, 1, -1), moe_w2[l],
                   moe_b2[l].reshape(N_EXPERTS, 1, -1))
    out = _combine_call(dest, h1, route, gate2, final_g.reshape(1, d), ys, T)
    return out.reshape(B, T, d)
```

```python
import functools

import jax
import jax.numpy as jnp
from jax import lax
from jax.experimental import pallas as pl
from jax.experimental.pallas import tpu as pltpu

F32 = jnp.float32
BF16 = jnp.bfloat16

D_MODEL = 1024
GLA_HEADS = 4
GLA_DK = 128
GLA_DV = 256
GLA_GATE_RANK = 16
GLA_TAU = 16.0
GLA_CHUNK = 64
NSA_HEADS = 8
NSA_KV_HEADS = 2
NSA_HD = 128
NSA_GROUP = NSA_HEADS // NSA_KV_HEADS
CMP_LEN = 32
CMP_STRIDE = 16
CMP_HIDDEN = 2 * NSA_HD
SEL_BLOCK = 64
SEL_TOPK = 16
WIN = 512
N_EXPERTS = 32
TOP_K = 4
D_EXPERT = D_MODEL
SWIGLU_LIMIT = 7.0
SWIGLU_ALPHA = 1.702
RMS_EPS = 1e-5

LANES = 128
NEG = -1e30
LOG2E = 1.4426950408889634
MOE_ROWS = 512
VMEM_LIMIT = 56 * 1024 * 1024

C_GQ, C_GK, C_GV, C_GR, C_NQ, C_MG, C_KV = 0, 512, 1024, 2048, 3072, 4096, 6144
MAIN_COLS = 7680
CMP_KV_COLS = 2 * NSA_KV_HEADS * NSA_HD
S_GA, S_NG = 0, 16

_O_GA = 3072
_O_NQ = 3088
_O_KV = 4112
_O_NG = 5648
_O_MG = 5672
_O_END = 7720


def _dot(a, b):
    return jnp.dot(a, b, preferred_element_type=F32)


def _dot_nt(a, b):
    return lax.dot_general(a, b, (((1,), (1,)), ((), ())), preferred_element_type=F32)


def _dot_tn(a, b):
    return lax.dot_general(a, b, (((0,), (0,)), ((), ())), preferred_element_type=F32)


def _split(a):
    hi = a.astype(BF16)
    lo = (a - hi.astype(F32)).astype(BF16)
    return hi, lo


def _dot3(a, b):
    ah, al = _split(a)
    bh, bl = _split(b)
    return _dot(ah, bh) + _dot(ah, bl) + _dot(al, bh)


def _sigmoid(x):
    return 1.0 / (1.0 + jnp.exp(-x))


def _params(*sem):
    return pltpu.CompilerParams(dimension_semantics=sem, vmem_limit_bytes=VMEM_LIMIT)


def _mod_kernel(c_ref, w_ref, b_ref, o_ref):
    c = c_ref[...]
    cond = c * _sigmoid(c)
    o_ref[...] = _dot3(cond, w_ref[...]) + b_ref[...]


def _mod_call(c8, w_ada, b_ada):
    d = c8.shape[1]
    n = w_ada.shape[1]
    return pl.pallas_call(
        _mod_kernel,
        grid=(n // d,),
        in_specs=[pl.BlockSpec((8, d), lambda j: (0, 0)),
                  pl.BlockSpec((d, d), lambda j: (0, j)),
                  pl.BlockSpec((1, d), lambda j: (0, j))],
        out_specs=pl.BlockSpec((8, d), lambda j: (0, j)),
        out_shape=jax.ShapeDtypeStruct((8, n), F32),
        compiler_params=_params("parallel"),
        name="mod",
    )(c8, w_ada, b_ada)


def _inproj_kernel(x_ref, g_ref, sc_ref, sh_ref, w_ref, ws_ref, o_ref, s_ref, kv_ref, u_sc, *, kv_off):
    @pl.when(pl.program_id(1) == 0)
    def _():
        xf = x_ref[...]
        ms = jnp.mean(xf * xf, axis=-1, keepdims=True)
        y = xf * lax.rsqrt(ms + RMS_EPS) * g_ref[...]
        u = (y * (1.0 + sc_ref[0]) + sh_ref[0]).astype(BF16)
        u_sc[...] = u
        s_ref[...] = _dot(u, ws_ref[...])

    acc = _dot(u_sc[...], w_ref[...])
    o_ref[...] = acc.astype(BF16)

    @pl.when(pl.program_id(1) == pl.num_programs(1) - 1)
    def _():
        kv_ref[...] = acc[:, kv_off:kv_off + CMP_KV_COLS]


def _inproj_call(x2, g, scale, shift, w_main, w_small, T, tm=1024, tn=3840):
    n, d = x2.shape
    tpb = T // tm
    ncol = MAIN_COLS // tn
    kv_off = C_KV - (ncol - 1) * tn
    assert kv_off >= 0 and kv_off % LANES == 0 and kv_off + CMP_KV_COLS <= tn
    return pl.pallas_call(
        functools.partial(_inproj_kernel, kv_off=kv_off),
        grid=(n // tm, ncol),
        in_specs=[pl.BlockSpec((tm, d), lambda i, j: (i, 0)),
                  pl.BlockSpec((1, d), lambda i, j: (0, 0)),
                  pl.BlockSpec((1, 1, d), lambda i, j: (i // tpb, 0, 0)),
                  pl.BlockSpec((1, 1, d), lambda i, j: (i // tpb, 0, 0)),
                  pl.BlockSpec((d, tn), lambda i, j: (0, j)),
                  pl.BlockSpec((d, LANES), lambda i, j: (0, 0))],
        out_specs=[pl.BlockSpec((tm, tn), lambda i, j: (i, j)),
                   pl.BlockSpec((tm, LANES), lambda i, j: (i, 0)),
                   pl.BlockSpec((tm, CMP_KV_COLS), lambda i, j: (i, 0))],
        out_shape=[jax.ShapeDtypeStruct((n, MAIN_COLS), BF16),
                   jax.ShapeDtypeStruct((n, LANES), F32),
                   jax.ShapeDtypeStruct((n, CMP_KV_COLS), F32)],
        scratch_shapes=[pltpu.VMEM((tm, d), BF16)],
        compiler_params=_params("parallel", "arbitrary"),
        name="inproj",
    )(x2, g, scale, shift, w_main, w_small)


def _gla_kernel(q_ref, k_ref, v_ref, r_ref, s_ref, wa_ref, ba_ref, g_ref, o_ref, st_sc, *, tt):
    @pl.when(pl.program_id(1) == 0)
    def _():
        st_sc[...] = jnp.zeros_like(st_sc)

    C = GLA_CHUNK
    nc = tt // C
    dk, dv = GLA_DK, GLA_DV
    ga = s_ref[:, S_GA:S_GA + GLA_GATE_RANK]
    z = _dot3(ga, wa_ref[...]) + ba_ref[...]
    la = (jnp.minimum(z, 0.0) - jnp.log(1.0 + jnp.exp(-jnp.abs(z)))) * (1.0 / GLA_TAU)
    row = lax.broadcasted_iota(jnp.int32, (tt, tt), 0)
    col = lax.broadcasted_iota(jnp.int32, (tt, tt), 1)
    causal = ((row // C) == (col // C)) & (col <= row)
    tril = jnp.where(causal, 1.0, 0.0).astype(BF16)
    hi, lo = _split(la)
    bc_all = _dot(tril, hi) + _dot(tril, lo)
    for h in range(GLA_HEADS):
        bc = bc_all[:, h * dk:(h + 1) * dk]
        bls = [bc[(c + 1) * C - 1:(c + 1) * C, :] for c in range(nc)]
        blr = jnp.concatenate([jnp.broadcast_to(b, (C, dk)) for b in bls], axis=0)
        q = q_ref[:, h * dk:(h + 1) * dk].astype(F32) * (dk ** -0.5)
        k = k_ref[:, h * dk:(h + 1) * dk].astype(F32)
        v = v_ref[:, h * dv:(h + 1) * dv]
        qe = (q * jnp.exp(bc)).astype(BF16)
        ke = (k * jnp.exp(-bc)).astype(BF16)
        kd = (k * jnp.exp(blr - bc)).astype(BF16)
        a = jnp.where(causal, _dot_nt(qe, ke), 0.0)
        intra = _dot(a.astype(BF16), v)
        st = st_sc[h]
        inter = []
        for c in range(nc):
            sl = slice(c * C, (c + 1) * C)
            inter.append(_dot_nt(qe[sl], st.astype(BF16)))
            st = st * jnp.exp(bls[c]) + _dot_tn(v[sl], kd[sl])
        st_sc[h] = st
        o = intra + jnp.concatenate(inter, axis=0)
        y = o * lax.rsqrt(jnp.mean(o * o, axis=-1, keepdims=True) + RMS_EPS) * g_ref[...]
        r = r_ref[:, h * dv:(h + 1) * dv].astype(F32)
        o_ref[:, h * dv:(h + 1) * dv] = (y * (r * _sigmoid(r))).astype(BF16)


def _gla_call(main, small, wa2, ba, norm_g, B, T, tt=256):
    n = main.shape[0]
    nt = T // tt
    H = GLA_HEADS
    rowmap = lambda b, t: b * nt + t
    return pl.pallas_call(
        functools.partial(_gla_kernel, tt=tt),
        grid=(B, nt),
        in_specs=[pl.BlockSpec((tt, H * GLA_DK), lambda b, t: (rowmap(b, t), C_GQ // (H * GLA_DK))),
                  pl.BlockSpec((tt, H * GLA_DK), lambda b, t: (rowmap(b, t), C_GK // (H * GLA_DK))),
                  pl.BlockSpec((tt, H * GLA_DV), lambda b, t: (rowmap(b, t), C_GV // (H * GLA_DV))),
                  pl.BlockSpec((tt, H * GLA_DV), lambda b, t: (rowmap(b, t), C_GR // (H * GLA_DV))),
                  pl.BlockSpec((tt, LANES), lambda b, t: (rowmap(b, t), 0)),
                  pl.BlockSpec((GLA_GATE_RANK, H * GLA_DK), lambda b, t: (0, 0)),
                  pl.BlockSpec((1, H * GLA_DK), lambda b, t: (0, 0)),
                  pl.BlockSpec((1, GLA_DV), lambda b, t: (0, 0))],
        out_specs=pl.BlockSpec((tt, H * GLA_DV), lambda b, t: (rowmap(b, t), 0)),
        out_shape=jax.ShapeDtypeStruct((n, H * GLA_DV), BF16),
        scratch_shapes=[pltpu.VMEM((H, GLA_DV, GLA_DK), F32)],
        compiler_params=_params("parallel", "arbitrary"),
        name="gla",
    )(main, main, main, main, small, wa2, ba, norm_g)


def _compress_kernel(x_ref, pe_ref, w1_ref, w2_ref, o_ref):
    half = CMP_STRIDE * NSA_HD
    nb = x_ref.shape[0] // CMP_STRIDE
    x = jnp.concatenate([x_ref[pl.ds(t, nb, stride=CMP_STRIDE), :].astype(BF16) for t in range(CMP_STRIDE)], axis=1)
    ylo = _dot(x, w1_ref[0, :half, :])
    yhi = _dot(x, w1_ref[0, half:, :])
    pe = jnp.broadcast_to(pe_ref[0], (8, 2 * half)).astype(BF16)
    pterm = _dot(pe, w1_ref[0])[0:1, :]
    pre = ylo + pltpu.roll(yhi, nb - 1, 0) + pterm
    hcu = pre * pre * pre
    hid = 0.5 * pre * (1.0 + jnp.tanh(0.7978845608028654 * (pre + 0.044715 * hcu)))
    o_ref[0, 0] = _dot(hid.astype(BF16), w2_ref[0]).astype(BF16)


def _compress_call(kv32, pe, w1, w2, B, T):
    G = NSA_KV_HEADS
    bg = B * G
    nb = T // CMP_STRIDE
    return pl.pallas_call(
        _compress_kernel,
        grid=(2, bg),
        in_specs=[pl.BlockSpec((T, NSA_HD), lambda s, i: (i // G, s * G + i % G)),
                  pl.BlockSpec((1, 1, CMP_LEN * NSA_HD), lambda s, i: (s, 0, 0)),
                  pl.BlockSpec((1, CMP_LEN * NSA_HD, CMP_HIDDEN), lambda s, i: (s, 0, 0)),
                  pl.BlockSpec((1, CMP_HIDDEN, NSA_HD), lambda s, i: (s, 0, 0))],
        out_specs=pl.BlockSpec((1, 1, nb, NSA_HD), lambda s, i: (s, i, 0, 0)),
        out_shape=jax.ShapeDtypeStruct((2, bg, nb, NSA_HD), BF16),
        compiler_params=_params("parallel", "parallel"),
        name="compress",
    )(kv32, pe, w1, w2)


def _cmpsel_kernel(q_ref, kc_ref, vc_ref, o_ref, sel_ref, *, tq, nsel):
    t0 = pl.program_id(2) * tq
    kc = kc_ref[0, 0]
    vc = vc_ref[0, 0]
    ncp = kc.shape[0]
    tpos = t0 + lax.broadcasted_iota(jnp.int32, (tq, 1), 0)
    cidx = lax.broadcasted_iota(jnp.int32, (1, ncp), 1)
    cmask = (cidx * CMP_STRIDE + (CMP_LEN - 1)) <= tpos
    psum = jnp.zeros((tq, ncp), F32)
    for r in range(NSA_GROUP):
        q = q_ref[:, r * NSA_HD:(r + 1) * NSA_HD]
        s = jnp.where(cmask, _dot_nt(q, kc) * (NSA_HD ** -0.5), NEG)
        m = jnp.max(s, axis=-1, keepdims=True)
        e = jnp.where(cmask, jnp.exp(s - m), 0.0)
        p = e / jnp.maximum(jnp.sum(e, axis=-1, keepdims=True), 1e-30)
        o_ref[:, r * NSA_HD:(r + 1) * NSA_HD] = _dot(p.astype(BF16), vc).astype(BF16)
        psum = psum + p
    ci = lax.broadcasted_iota(jnp.int32, (nsel, ncp), 1) * CMP_STRIDE
    si = lax.broadcasted_iota(jnp.int32, (nsel, ncp), 0) * SEL_BLOCK
    ov = jnp.where((ci < si + SEL_BLOCK) & (ci + CMP_LEN > si), 1.0, 0.0).astype(BF16)
    hi, lo = _split(psum)
    imp = _dot_nt(ov, hi) + _dot_nt(ov, lo)
    tlane = t0 + lax.broadcasted_iota(jnp.int32, (1, tq), 1)
    jblk = lax.broadcasted_iota(jnp.int32, (nsel, 1), 0)
    tblk = tlane // SEL_BLOCK
    forced = (jblk == 0) | (jblk == tblk) | (jblk == tblk - 1)
    valid = (jblk * SEL_BLOCK) <= tlane
    score = jnp.where(forced, jnp.inf, jnp.where(valid, imp, -jnp.inf))
    ngrp = nsel // 8
    groups = [score[8 * a:8 * a + 8, :] for a in range(ngrp)]
    sub = lax.broadcasted_iota(jnp.int32, (8, 1), 0)
    ranks = [jnp.zeros((8, tq), F32) for _ in range(ngrp)]
    for i in range(nsel):
        si_ = score[i:i + 1, :]
        for a in range(ngrp):
            if a < i // 8:
                beats = si_ > groups[a]
            elif a > i // 8:
                beats = si_ >= groups[a]
            else:
                beats = (si_ > groups[a]) | ((si_ == groups[a]) & (sub > i % 8))
            ranks[a] = ranks[a] + jnp.where(beats, 1.0, 0.0)
    rank = jnp.concatenate(ranks, axis=0)
    ntop = min(SEL_TOPK, nsel)
    chosen = jnp.where(valid & (rank < ntop), 1.0, 0.0)
    sel_ref[0, 0] = jnp.transpose(chosen).astype(BF16)


def _cmpsel_call(main, kvc, B, T, tq=512):
    n = main.shape[0]
    nq = T // tq
    nsel = T // SEL_BLOCK
    ncp = kvc.shape[2]
    G = NSA_KV_HEADS
    qw = NSA_GROUP * NSA_HD
    return pl.pallas_call(
        functools.partial(_cmpsel_kernel, tq=tq, nsel=nsel),
        grid=(B, G, nq),
        in_specs=[pl.BlockSpec((tq, qw), lambda b, g, t: (b * nq + t, C_NQ // qw + g)),
                  pl.BlockSpec((1, 1, ncp, NSA_HD), lambda b, g, t: (0, b * G + g, 0, 0)),
                  pl.BlockSpec((1, 1, ncp, NSA_HD), lambda b, g, t: (1, b * G + g, 0, 0))],
        out_specs=[pl.BlockSpec((tq, qw), lambda b, g, t: (b * nq + t, g)),
                   pl.BlockSpec((1, 1, tq, nsel), lambda b, g, t: (b, g, t, 0))],
        out_shape=[jax.ShapeDtypeStruct((n, NSA_HEADS * NSA_HD), BF16),
                   jax.ShapeDtypeStruct((B, G, T, nsel), BF16)],
        compiler_params=_params("parallel", "parallel", "parallel"),
        name="cmpsel",
    )(main, kvc, kvc)


def _nsa_kernel(q_ref, ks_ref, vs_ref, kw_ref, vw_ref, sel_ref, oc_ref, s_ref, o_ref, s_sc, *, tq, tk, T, nsel):
    R = NSA_GROUP
    hd = NSA_HD
    g = pl.program_id(1)
    t0 = pl.program_id(2) * tq
    qscale = (hd ** -0.5) * LOG2E
    q4 = jnp.concatenate([q_ref[:, r * hd:(r + 1) * hd] for r in range(R)], axis=0)
    q4 = (q4.astype(F32) * qscale).astype(BF16)
    rowpos = t0 + lax.broadcasted_iota(jnp.int32, (tq, 1), 0)
    selm = sel_ref[0, 0].astype(F32) - 1.0
    selm4 = jnp.concatenate([selm] * R, axis=0).astype(BF16)

    qa = jnp.concatenate([q4, selm4], axis=1)
    jcol = lax.broadcasted_iota(jnp.int32, (tk, nsel), 1)
    crow = lax.broadcasted_iota(jnp.int32, (tk, nsel), 0) // SEL_BLOCK
    kcol = lax.broadcasted_iota(jnp.int32, (1, tk), 1)

    def scores(kt):
        k0 = pl.multiple_of(kt * tk, tk)
        ind = jnp.where(jcol == crow + kt * (tk // SEL_BLOCK), 1e30, 0.0).astype(BF16)
        s_sc[kt % 2] = _dot_nt(qa, jnp.concatenate([ks_ref[pl.ds(k0, tk), :], ind], axis=1))

    def attend(kt, carry, diagonal):
        m, l, acc = carry
        k0 = pl.multiple_of(kt * tk, tk)
        s = s_sc[kt % 2].reshape(R, tq, tk)
        if diagonal:
            s = s + jnp.where((k0 + kcol) <= rowpos, 0.0, NEG)[None]
        m_new = jnp.maximum(m, jnp.max(s, axis=-1, keepdims=True))
        alpha = jnp.exp2(m - m_new)
        p = jnp.exp2(s - m_new)
        l = alpha * l + jnp.sum(p, axis=-1, keepdims=True)
        pv = _dot(p.reshape(R * tq, tk).astype(BF16), vs_ref[pl.ds(k0, tk), :]).reshape(R, tq, hd)
        return m_new, l, alpha * acc + pv

    def step(kt, carry):
        carry = attend(kt, carry, False)
        scores(kt + 1)
        return carry

    init = (jnp.full((R, tq, 1), NEG, F32), jnp.zeros((R, tq, 1), F32), jnp.zeros((R, tq, hd), F32))
    kdiag = t0 // tk
    scores(0)
    carry = lax.fori_loop(0, kdiag, step, init)
    _, l_s, acc_s = attend(kdiag, carry, True)
    o_sel = acc_s / l_s

    wlen = WIN + tq
    w0 = pl.multiple_of(jnp.maximum(t0 - WIN, 0), tq)
    kw = kw_ref[pl.ds(w0, wlen), :]
    vw = vw_ref[pl.ds(w0, wlen), :]
    kpos = w0 + lax.broadcasted_iota(jnp.int32, (1, wlen), 1)
    wbias = jnp.where((kpos <= rowpos) & (rowpos - kpos < WIN), 0.0, NEG)
    sw = _dot_nt(q4, kw).reshape(R, tq, wlen) + wbias[None]
    mw = jnp.max(sw, axis=-1, keepdims=True)
    pw = jnp.exp2(sw - mw)
    lw = jnp.sum(pw, axis=-1, keepdims=True)
    o_win = _dot(pw.reshape(R * tq, wlen).astype(BF16), vw).reshape(R, tq, hd) / lw

    ng = R * 3
    graw = jnp.where(g == 0, s_ref[:, S_NG:S_NG + ng], s_ref[:, S_NG + ng:S_NG + 2 * ng])
    gates = _sigmoid(graw)
    for r in range(R):
        oc = oc_ref[:, r * hd:(r + 1) * hd].astype(F32)
        out = (gates[:, 3 * r:3 * r + 1] * oc + gates[:, 3 * r + 1:3 * r + 2] * o_sel[r]
               + gates[:, 3 * r + 2:3 * r + 3] * o_win[r])
        o_ref[:, r * hd:(r + 1) * hd] = out.astype(BF16)


def _nsa_call(main, small, sel, o_cmp, B, T, tq=128, tk=1024):
    n = main.shape[0]
    nq = T // tq
    nsel = T // SEL_BLOCK
    G = NSA_KV_HEADS
    hd = NSA_HD
    qw = NSA_GROUP * hd
    tk = min(tk, T)
    kvb = C_KV // hd

    def kvspec(slot):
        return pl.BlockSpec((T, hd), lambda b, g, t: (b, kvb + 2 * slot + g))

    return pl.pallas_call(
        functools.partial(_nsa_kernel, tq=tq, tk=tk, T=T, nsel=nsel),
        grid=(B, G, nq),
        in_specs=[pl.BlockSpec((tq, qw), lambda b, g, t: (b * nq + t, C_NQ // qw + g)),
                  kvspec(2), kvspec(3), kvspec(4), kvspec(5),
                  pl.BlockSpec((1, 1, tq, nsel), lambda b, g, t: (b, g, t, 0)),
                  pl.BlockSpec((tq, qw), lambda b, g, t: (b * nq + t, g)),
                  pl.BlockSpec((tq, LANES), lambda b, g, t: (b * nq + t, 0))],
        out_specs=pl.BlockSpec((tq, qw), lambda b, g, t: (b * nq + t, g)),
        out_shape=jax.ShapeDtypeStruct((n, NSA_HEADS * hd), BF16),
        scratch_shapes=[pltpu.VMEM((2, NSA_GROUP * tq, tk), F32)],
        compiler_params=_params("parallel", "parallel", "parallel"),
        name="nsa",
    )(main, main, main, main, main, sel, o_cmp, small)


def _merge_kernel(yg_ref, yn_ref, mg_ref, x_ref, g1_ref, sc_ref, sh_ref, n2_ref, pg_ref, pn_ref, wo_ref,
                  rw_ref, rb_ref, h_ref, u_ref, rt_ref, cnt_ref, carry_sc, *, tm):
    d = D_MODEL

    @pl.when(pl.program_id(0) == 0)
    def _():
        carry_sc[...] = jnp.zeros_like(carry_sc)

    a = _dot(yg_ref[...], pg_ref[...])
    b = _dot(yn_ref[...], pn_ref[...])
    mg = _sigmoid(mg_ref[...].astype(F32))
    merged = mg[:, :d] * a + mg[:, d:] * b
    mix = _dot(merged.astype(BF16), wo_ref[...])
    h = x_ref[...] + g1_ref[0] * mix
    h_ref[...] = h
    y = h * lax.rsqrt(jnp.mean(h * h, axis=-1, keepdims=True) + RMS_EPS) * n2_ref[...]
    u = y * (1.0 + sc_ref[0]) + sh_ref[0]
    u_ref[...] = u

    lane = lax.broadcasted_iota(jnp.int32, (1, LANES), 1).astype(F32)
    logits = jnp.where(lane < N_EXPERTS, _dot3(u, rw_ref[...]) + rb_ref[...], -jnp.inf)
    vals, idxs = [], []
    cur = logits
    for _ in range(TOP_K):
        m = jnp.max(cur, axis=-1, keepdims=True)
        idx = jnp.min(jnp.where(cur == m, lane, float(LANES)), axis=-1, keepdims=True)
        vals.append(m)
        idxs.append(idx)
        cur = jnp.where(lane == idx, -jnp.inf, cur)
    es = [jnp.exp(v - vals[0]) for v in vals]
    den = es[0] + es[1] + es[2] + es[3]
    onehots = [lane == idx for idx in idxs]
    oh = jnp.zeros((tm, LANES), F32)
    for o in onehots:
        oh = oh + jnp.where(o, 1.0, 0.0)
    row = lax.broadcasted_iota(jnp.int32, (tm, tm), 0)
    col = lax.broadcasted_iota(jnp.int32, (tm, tm), 1)
    tril = jnp.where(col <= row, 1.0, 0.0).astype(BF16)
    cnt = _dot(tril, oh.astype(BF16))
    tot = cnt + carry_sc[0:1, :]
    route = jnp.zeros((tm, LANES), F32)
    for k in range(TOP_K):
        pos = jnp.sum(jnp.where(onehots[k], tot - 1.0, 0.0), axis=-1, keepdims=True)
        route = jnp.where(lane == k, es[k] / den, route)
        route = jnp.where(lane == TOP_K + k, idxs[k], route)
        route = jnp.where(lane == 2 * TOP_K + k, pos, route)
    rt_ref[...] = route
    newc = tot[tm - 1:tm, :]
    carry_sc[...] = jnp.broadcast_to(newc, carry_sc.shape)
    cnt_ref[...] = jnp.broadcast_to(newc, cnt_ref.shape)


def _merge_call(y_gla, y_nsa, main, x2, gate1, scale2, shift2, n2g, pg, pn, wo, rw, rb, T, tm=512):
    n, d = x2.shape
    tpb = T // tm
    bmap = lambda i: (i // tpb, 0, 0)
    full = lambda i: (0, 0)
    return pl.pallas_call(
        functools.partial(_merge_kernel, tm=tm),
        grid=(n // tm,),
        in_specs=[pl.BlockSpec((tm, d), lambda i: (i, 0)),
                  pl.BlockSpec((tm, d), lambda i: (i, 0)),
                  pl.BlockSpec((tm, 2 * d), lambda i: (i, C_MG // (2 * d))),
                  pl.BlockSpec((tm, d), lambda i: (i, 0)),
                  pl.BlockSpec((1, 1, d), bmap), pl.BlockSpec((1, 1, d), bmap), pl.BlockSpec((1, 1, d), bmap),
                  pl.BlockSpec((1, d), full),
                  pl.BlockSpec((d, d), full), pl.BlockSpec((d, d), full), pl.BlockSpec((d, d), full),
                  pl.BlockSpec((d, LANES), full), pl.BlockSpec((1, LANES), full)],
        out_specs=[pl.BlockSpec((tm, d), lambda i: (i, 0)),
                   pl.BlockSpec((tm, d), lambda i: (i, 0)),
                   pl.BlockSpec((tm, LANES), lambda i: (i, 0)),
                   pl.BlockSpec((8, LANES), full)],
        out_shape=[jax.ShapeDtypeStruct((n, d), F32), jax.ShapeDtypeStruct((n, d), F32),
                   jax.ShapeDtypeStruct((n, LANES), F32), jax.ShapeDtypeStruct((8, LANES), F32)],
        scratch_shapes=[pltpu.VMEM((8, LANES), F32)],
        compiler_params=_params("arbitrary"),
        name="merge",
    )(y_gla, y_nsa, main, x2, gate1, scale2, shift2, n2g, pg, pn, wo, rw, rb)


SUBLANES = 8
ROW_TILES = D_MODEL // LANES
assert ROW_TILES == SUBLANES


def _to_token_tiles(dst_ref, src, rows):
    for j in range(ROW_TILES):
        dst_ref[pl.ds(j, rows, stride=SUBLANES), :] = src[:, j * LANES:(j + 1) * LANES]


def _from_token_tiles(src_ref, rows):
    return [src_ref[pl.ds(j, rows, stride=SUBLANES), :] for j in range(ROW_TILES)]


def _dispatch_kernel(dest_ref, pend_ref, u_ref, xs_ref, xt, zbuf, sem, zsem, *, td):
    base = pl.program_id(0) * (td * TOP_K)
    blk = MOE_ROWS * SUBLANES

    @pl.when(pl.program_id(0) == 0)
    def _():
        zbuf[...] = jnp.zeros_like(zbuf)

        def last_block(e):
            end = pend_ref[e]
            prev = pend_ref[e - 1] if e else 0
            start = pl.multiple_of(jnp.maximum(end - MOE_ROWS, 0) * SUBLANES, blk)
            return end > prev, pltpu.make_async_copy(zbuf, xs_ref.at[pl.ds(start, blk)], zsem)

        def unused_block(b):
            return b * MOE_ROWS >= pend_ref[N_EXPERTS - 1], pltpu.make_async_copy(
                zbuf, xs_ref.at[pl.ds(b * blk, blk)], zsem)

        nblk = xs_ref.shape[0] // blk
        fills = [functools.partial(last_block, e) for e in range(N_EXPERTS)]
        fills += [functools.partial(unused_block, b) for b in range(nblk - N_EXPERTS, nblk)]
        for f in fills:
            cond, fill = f()
            pl.when(cond)(fill.start)
        for f in fills:
            cond, fill = f()
            pl.when(cond)(fill.wait)

    _to_token_tiles(xt, u_ref[...], td)

    def row_copy(r, slot):
        return pltpu.make_async_copy(xt.at[pl.ds(pl.multiple_of(r * SUBLANES, SUBLANES), SUBLANES)],
                                     xs_ref.at[pl.ds(pl.multiple_of(slot * SUBLANES, SUBLANES), SUBLANES)], sem)

    def issue(r, c):
        for k in range(TOP_K):
            row_copy(r, dest_ref[base + r * TOP_K + k]).start(priority=k % 2)
        return c

    def drain(r, c):
        for k in range(TOP_K):
            row_copy(r, dest_ref[base + r * TOP_K + k]).wait()
        return c

    lax.fori_loop(0, td, issue, 0)
    lax.fori_loop(0, td, drain, 0)


def _dispatch_call(dest, pends, u2, p_rows, td=512):
    n, d = u2.shape
    return pl.pallas_call(
        functools.partial(_dispatch_kernel, td=td),
        grid_spec=pltpu.PrefetchScalarGridSpec(
            num_scalar_prefetch=2,
            grid=(n // td,),
            in_specs=[pl.BlockSpec((td, d), lambda i, dest, pends: (i, 0))],
            out_specs=pl.BlockSpec(memory_space=pl.ANY),
            scratch_shapes=[pltpu.VMEM((td * SUBLANES, LANES), F32), pltpu.VMEM((MOE_ROWS * SUBLANES, LANES), F32),
                            pltpu.SemaphoreType.DMA(()), pltpu.SemaphoreType.DMA(())]),
        out_shape=jax.ShapeDtypeStruct((p_rows * SUBLANES, LANES), F32),
        compiler_params=_params("arbitrary"),
        name="dispatch",
    )(dest, pends, u2)


def _moe_kernel(be_ref, na_ref, x_ref, w1_ref, b1_ref, w2_ref, b2_ref, y_ref):
    del be_ref
    active = pl.program_id(0) < na_ref[0]
    f = D_EXPERT
    m = MOE_ROWS

    @pl.when(active)
    def _():
        x = jnp.concatenate([t.astype(BF16) for t in _from_token_tiles(x_ref, m)], axis=1)
        cw = 256
        hb = jnp.concatenate([_dot(x, w1_ref[0, :, c:c + cw].astype(BF16)) for c in range(0, 2 * f, cw)], axis=1)
        hb = hb + b1_ref[0]
        gate = jnp.minimum(hb[:, :f], SWIGLU_LIMIT)
        up = jnp.clip(hb[:, f:], -SWIGLU_LIMIT, SWIGLU_LIMIT)
        act = (gate * _sigmoid(SWIGLU_ALPHA * gate) * (up + 1.0)).astype(BF16)
        y = jnp.concatenate([_dot(act, w2_ref[0, :, c:c + cw].astype(BF16)) for c in range(0, D_MODEL, cw)], axis=1)
        _to_token_tiles(y_ref, y + b2_ref[0], m)

    @pl.when(jnp.logical_not(active))
    def _():
        y_ref[...] = jnp.zeros_like(y_ref)


def _moe_call(blk_e, nact, xs, w1, b1, w2, b2):
    d, f2 = w1.shape[1], w1.shape[2]
    blk = MOE_ROWS * SUBLANES
    return pl.pallas_call(
        _moe_kernel,
        grid_spec=pltpu.PrefetchScalarGridSpec(
            num_scalar_prefetch=2,
            grid=(xs.shape[0] // blk,),
            in_specs=[pl.BlockSpec((blk, LANES), lambda i, be, na: (i, 0)),
                      pl.BlockSpec((1, d, f2), lambda i, be, na: (be[i], 0, 0)),
                      pl.BlockSpec((1, 1, f2), lambda i, be, na: (be[i], 0, 0)),
                      pl.BlockSpec((1, f2 // 2, d), lambda i, be, na: (be[i], 0, 0)),
                      pl.BlockSpec((1, 1, d), lambda i, be, na: (be[i], 0, 0))],
            out_specs=pl.BlockSpec((blk, LANES), lambda i, be, na: (i, 0))),
        out_shape=jax.ShapeDtypeStruct(xs.shape, F32),
        compiler_params=_params("arbitrary"),
        name="moe",
    )(blk_e, nact, xs, w1, b1, w2, b2)


def _combine_kernel(dest_ref, h_ref, rt_ref, g2_ref, fg_ref, ys_ref, o_ref, buf, sems, *, tc):
    i = pl.program_id(0)
    slot = i % 2

    def row_copy(step, sl, r, k):
        src = dest_ref[step * (tc * TOP_K) + r * TOP_K + k]
        return pltpu.make_async_copy(
            ys_ref.at[pl.ds(pl.multiple_of(src * SUBLANES, SUBLANES), SUBLANES)],
            buf.at[sl, k, pl.ds(pl.multiple_of(r * SUBLANES, SUBLANES), SUBLANES)], sems.at[sl])

    def issue(step, sl):
        def body(r, c):
            for k in range(TOP_K):
                row_copy(step, sl, r, k).start(priority=k % 2)
            return c
        lax.fori_loop(0, tc, body, 0)

    def drain(step, sl):
        def body(r, c):
            for k in range(TOP_K):
                row_copy(step, sl, r, k).wait()
            return c
        lax.fori_loop(0, tc, body, 0)

    @pl.when(i == 0)
    def _():
        issue(0, 0)

    @pl.when(i + 1 < pl.num_programs(0))
    def _():
        issue(i + 1, 1 - slot)

    drain(i, slot)
    rt = rt_ref[...]
    slabs = [_from_token_tiles(buf.at[slot, k], tc) for k in range(TOP_K)]
    cols = []
    for j in range(ROW_TILES):
        col = rt[:, 0:1] * slabs[0][j]
        for k in range(1, TOP_K):
            col = col + rt[:, k:k + 1] * slabs[k][j]
        cols.append(col)
    moe = jnp.concatenate(cols, axis=1)
    hh = h_ref[...] + g2_ref[0] * moe
    o_ref[...] = hh * lax.rsqrt(jnp.mean(hh * hh, axis=-1, keepdims=True) + RMS_EPS) * fg_ref[...]


def _combine_call(dest, h, route, gate2, fg, ys, T, tc=256):
    n, d = h.shape
    tpb = T // tc
    return pl.pallas_call(
        functools.partial(_combine_kernel, tc=tc),
        grid_spec=pltpu.PrefetchScalarGridSpec(
            num_scalar_prefetch=1,
            grid=(n // tc,),
            in_specs=[pl.BlockSpec((tc, d), lambda i, dest: (i, 0)),
                      pl.BlockSpec((tc, LANES), lambda i, dest: (i, 0)),
                      pl.BlockSpec((1, 1, d), lambda i, dest: (i // tpb, 0, 0)),
                      pl.BlockSpec((1, d), lambda i, dest: (0, 0)),
                      pl.BlockSpec(memory_space=pl.ANY)],
            out_specs=pl.BlockSpec((tc, d), lambda i, dest: (i, 0)),
            scratch_shapes=[pltpu.VMEM((2, TOP_K, tc * SUBLANES, LANES), F32), pltpu.SemaphoreType.DMA((2,))]),
        out_shape=jax.ShapeDtypeStruct((n, d), F32),
        compiler_params=_params("arbitrary"),
        name="combine",
    )(dest, h, route, gate2, fg, ys)


def _mixer(x2, mod, norm1_g, w_in, gla_wa2, gla_ba, gla_norm_g, cmp_pe, cmp_w1, cmp_w2, B, T):
    d = D_MODEL
    G = NSA_KV_HEADS
    hd = NSA_HD
    shift1 = mod[:, 0 * d:1 * d].reshape(B, 1, d)
    scale1 = mod[:, 1 * d:2 * d].reshape(B, 1, d)
    w_main = jnp.concatenate([w_in[:, :_O_GA], w_in[:, _O_NQ:_O_KV], w_in[:, _O_MG:_O_END],
                              w_in[:, _O_KV:_O_NG]], axis=1).astype(BF16)
    w_small = jnp.concatenate([w_in[:, _O_GA:_O_NQ], w_in[:, _O_NG:_O_MG],
                               jnp.zeros((d, LANES - GLA_GATE_RANK - NSA_HEADS * 3), F32)], axis=1).astype(BF16)
    main, small, kv32 = _inproj_call(x2, norm1_g.reshape(1, d), scale1, shift1, w_main, w_small, T)
    y_gla = _gla_call(main, small, gla_wa2, gla_ba.reshape(1, -1), gla_norm_g.reshape(1, -1), B, T)
    kvc = _compress_call(kv32, cmp_pe.reshape(2, 1, CMP_LEN * hd), cmp_w1.astype(BF16), cmp_w2.astype(BF16), B, T)
    o_cmp, sel = _cmpsel_call(main, kvc, B, T)
    y_nsa = _nsa_call(main, small, sel, o_cmp, B, T)
    return main, y_gla, y_nsa


def _route_tables(route, counts, n):
    m = MOE_ROWS
    e = route[:, TOP_K:2 * TOP_K].astype(jnp.int32)
    pos = route[:, 2 * TOP_K:3 * TOP_K].astype(jnp.int32)
    cnt = counts[0, :N_EXPERTS].astype(jnp.int32)
    padded = ((cnt + m - 1) // m) * m
    pends = jnp.cumsum(padded)
    pstarts = pends - padded
    onehot = e[..., None] == jnp.arange(N_EXPERTS, dtype=jnp.int32)
    dest = (jnp.sum(jnp.where(onehot, pstarts, 0), axis=-1) + pos).reshape(-1)
    p_rows = n * TOP_K + N_EXPERTS * m
    nblk = p_rows // m
    starts = jnp.arange(nblk, dtype=jnp.int32) * m
    blk_e = jnp.minimum(jnp.sum((pends[None, :] <= starts[:, None]).astype(jnp.int32), axis=1), N_EXPERTS - 1)
    nact = (pends[-1:] // m).astype(jnp.int32)
    return dest, pends.astype(jnp.int32), blk_e, nact, p_rows


def kernel(x, c, w_ada, b_ada, norm1_g, w_in, gla_wa2, gla_ba, gla_norm_g, cmp_pe, cmp_w1, cmp_w2, proj_gla,
           proj_nsa, w_out, norm2_g, router_w, router_b, moe_w1, moe_b1, moe_w2, moe_b2, final_g):
    B, T, d = x.shape
    n = B * T
    assert w_ada.shape[0] == 1 and d == D_MODEL and B <= 8, "single-layer block of width D_MODEL"
    l = 0
    h = x.reshape(n, d)
    c8 = jnp.concatenate([c, jnp.zeros((8 - B, d), c.dtype)], axis=0)
    mod = _mod_call(c8, w_ada[l], b_ada[l].reshape(1, -1))[:B]
    gate1 = mod[:, 2 * d:3 * d].reshape(B, 1, d)
    shift2 = mod[:, 3 * d:4 * d].reshape(B, 1, d)
    scale2 = mod[:, 4 * d:5 * d].reshape(B, 1, d)
    gate2 = mod[:, 5 * d:6 * d].reshape(B, 1, d)
    main, y_gla, y_nsa = _mixer(h, mod, norm1_g[l], w_in[l], gla_wa2[l], gla_ba[l], gla_norm_g[l],
                                cmp_pe[l], cmp_w1[l], cmp_w2[l], B, T)
    rw = jnp.concatenate([router_w[l], jnp.zeros((d, LANES - N_EXPERTS), F32)], axis=1)
    rb = jnp.concatenate([router_b[l], jnp.zeros((LANES - N_EXPERTS,), F32)]).reshape(1, LANES)
    h1, u2, route, counts = _merge_call(
        y_gla, y_nsa, main, h, gate1, scale2, shift2, norm2_g[l].reshape(1, d),
        proj_gla[l].astype(BF16), proj_nsa[l].astype(BF16), w_out[l].astype(BF16), rw, rb, T)
    dest, pends, blk_e, nact, p_rows = _route_tables(route, counts, n)
    xs = _dispatch_call(dest, pends, u2, p_rows)
    ys = _moe_call(blk_e, nact, xs, moe_w1[l], moe_b1[l].reshape(N_EXPERTS, 1, -1), moe_w2[l],
                   moe_b2[l].reshape(N_EXPERTS, 1, -1))
    out = _combine_call(dest, h1, route, gate2, final_g.reshape(1, d), ys, T)
    return out.reshape(B, T, d)
```
